```python
import math
import jax, jax.numpy as jnp
from jax import lax
import numpy as np

D_MODEL = 1024
BATCH = 8
SEQ = 4096
DEPTH = 1

HEAD_DIM = 64
N_HEADS = D_MODEL // HEAD_DIM
N_KV_HEADS = N_HEADS // 4
ATTN_WIDTH = N_HEADS * HEAD_DIM
KV_WIDTH = N_KV_HEADS * HEAD_DIM
WINDOW = 128
BLOCK = 128
ROT_DIM = HEAD_DIM // 4
ROPE_THETA = 500000.0
RNN_WIDTH = D_MODEL
RNN_BLOCK_WIDTH = 256
RNN_N_BLOCKS = RNN_WIDTH // RNN_BLOCK_WIDTH
LRU_C = 8.0
CONV_WIDTH = 4
NORM_EPS = 1e-6
SPLIT_SIZES = (ATTN_WIDTH, KV_WIDTH, KV_WIDTH, ATTN_WIDTH,
               RNN_WIDTH, RNN_WIDTH,
               D_MODEL, D_MODEL)
IN_WIDTH = sum(SPLIT_SIZES)
SPLIT_POINTS = tuple(int(v) for v in np.cumsum(SPLIT_SIZES)[:-1])

kernel_name = "hybrid_swa_sink_rglru_gated_block"


def rmsnorm(x, g):
    xf = x.astype(jnp.float32)
    r = lax.rsqrt(jnp.mean(xf * xf, axis=-1, keepdims=True) + NORM_EPS)
    return (xf * r).astype(x.dtype) * g


def partial_rope(t, pos):
    half = ROT_DIM // 2
    inv_freq = ROPE_THETA ** (-jnp.arange(0, ROT_DIM, 2, dtype=jnp.float32) / ROT_DIM)
    ang = pos[..., None].astype(jnp.float32) * inv_freq
    cos = jnp.cos(ang)[:, :, None, :]
    sin = jnp.sin(ang)[:, :, None, :]
    rot = t[..., :ROT_DIM].astype(jnp.float32)
    x1, x2 = rot[..., :half], rot[..., half:]
    rotated = jnp.concatenate([x1 * cos - x2 * sin, x2 * cos + x1 * sin], axis=-1)
    return jnp.concatenate([rotated.astype(t.dtype), t[..., ROT_DIM:]], axis=-1)


def sliding_window_attention_with_sinks(q, k, v, sinks):
    B, S, H, hd = q.shape
    nb = S // BLOCK
    G = H // N_KV_HEADS
    qb = q.reshape(B, nb, BLOCK, N_KV_HEADS, G, hd).astype(jnp.float32)
    pad = ((0, 0), (BLOCK, 0), (0, 0), (0, 0))
    kp = jnp.pad(k, pad).reshape(B, nb + 1, BLOCK, N_KV_HEADS, hd)
    vp = jnp.pad(v, pad).reshape(B, nb + 1, BLOCK, N_KV_HEADS, hd)
    kb = jnp.concatenate([kp[:, :-1], kp[:, 1:]], axis=2).astype(jnp.float32)
    vb = jnp.concatenate([vp[:, :-1], vp[:, 1:]], axis=2).astype(jnp.float32)
    s = jnp.einsum('bnqkgd,bnskd->bnkgqs', qb, kb) * (1.0 / math.sqrt(hd))
    qi = jnp.arange(BLOCK)[:, None]
    kj = jnp.arange(2 * BLOCK)[None, :]
    diff = qi + BLOCK - kj
    band = (diff >= 0) & (diff < WINDOW)
    kpos = jnp.arange(nb)[:, None] * BLOCK - BLOCK + jnp.arange(2 * BLOCK)[None, :]
    mask = band[None] & (kpos >= 0)[:, None, :]
    s = jnp.where(mask[None, :, None, None], s, -1e30)
    sink = sinks.astype(jnp.float32).reshape(N_KV_HEADS, G)[None, None, :, :, None, None]
    m = jnp.maximum(jnp.max(s, axis=-1, keepdims=True), sink)
    p = jnp.exp(s - m)
    denom = jnp.sum(p, axis=-1, keepdims=True) + jnp.exp(sink - m)
    o = jnp.einsum('bnkgqs,bnskd->bnqkgd', p / denom, vb)
    return o.reshape(B, S, H * hd).astype(q.dtype)


def causal_depthwise_conv(x, w, b):
    S = x.shape[1]
    xp = jnp.pad(x, ((0, 0), (CONV_WIDTH - 1, 0), (0, 0)))
    y = sum(xp[:, k:k + S] * w[k] for k in range(CONV_WIDTH))
    return y + b


def rg_lru(xr, pos, wa, ba, wx, bx, lam):
    B, S, D = xr.shape
    xb = xr.reshape(B, S, RNN_N_BLOCKS, RNN_BLOCK_WIDTH)
    r = jax.nn.sigmoid(jnp.einsum('bshi,hij->bshj', xb, wa).reshape(B, S, D) + ba)
    i = jax.nn.sigmoid(jnp.einsum('bshi,hij->bshj', xb, wx).reshape(B, S, D) + bx)
    log_a = -LRU_C * r.astype(jnp.float32) * jax.nn.softplus(-lam.astype(jnp.float32))
    a = jnp.exp(log_a)
    mult = jnp.sqrt(-jnp.expm1(2.0 * log_a))
    reset = (pos == 0)[..., None]
    mult = jnp.where(reset, 1.0, mult)
    a = jnp.where(reset, 0.0, a)
    b = mult * (i * xr).astype(jnp.float32)

    def combine(lhs, rhs):
        a1, b1 = lhs
        a2, b2 = rhs
        return a1 * a2, a2 * b1 + b2

    _, h = lax.associative_scan(combine, (a, b), axis=1)
    return h.astype(xr.dtype)


def setup_inputs(seed: int = 0) -> dict:
    key = jax.random.key(seed)
    ks = jax.random.split(key, 20)
    f32 = jnp.float32
    nrm = lambda k, shape, scale: jax.random.normal(k, shape, f32) * scale
    s = jax.nn.sigmoid(jnp.zeros(()))
    del s
    a_c = jax.random.uniform(ks[12], (DEPTH, RNN_WIDTH), f32, 0.9, 0.999)
    a_base = a_c ** (1.0 / LRU_C)
    lam = jnp.log(a_base) - jnp.log1p(-a_base)
    return {
        "x": nrm(ks[0], (BATCH, SEQ, D_MODEL), 1.0),
        "c": nrm(ks[1], (BATCH, D_MODEL), 1.0),
        "positions": jnp.broadcast_to(jnp.arange(SEQ, dtype=jnp.int32), (BATCH, SEQ)),
        "w_ada": nrm(ks[2], (DEPTH, D_MODEL, 3 * D_MODEL), 0.1 * D_MODEL ** -0.5),
        "b_ada": nrm(ks[3], (DEPTH, 3 * D_MODEL), 0.01),
        "norm_g": 1.0 + nrm(ks[4], (DEPTH, D_MODEL), 0.02),
        "w_in": nrm(ks[5], (DEPTH, D_MODEL, IN_WIDTH), D_MODEL ** -0.5),
        "attn_sinks": nrm(ks[6], (DEPTH, N_HEADS), 0.5),
        "conv_w": nrm(ks[7], (DEPTH, CONV_WIDTH, RNN_WIDTH), CONV_WIDTH ** -0.5),
        "conv_b": nrm(ks[8], (DEPTH, RNN_WIDTH), 0.01),
        "rg_wa": nrm(ks[9], (DEPTH, RNN_N_BLOCKS, RNN_BLOCK_WIDTH, RNN_BLOCK_WIDTH), RNN_BLOCK_WIDTH ** -0.5),
        "rg_ba": nrm(ks[10], (DEPTH, RNN_WIDTH), 0.01),
        "rg_wx": nrm(ks[11], (DEPTH, RNN_N_BLOCKS, RNN_BLOCK_WIDTH, RNN_BLOCK_WIDTH), RNN_BLOCK_WIDTH ** -0.5),
        "rg_bx": nrm(ks[13], (DEPTH, RNN_WIDTH), 0.01),
        "rg_lambda": lam,
        "w_attn_proj": nrm(ks[14], (DEPTH, ATTN_WIDTH, D_MODEL), ATTN_WIDTH ** -0.5),
        "w_rnn_proj": nrm(ks[15], (DEPTH, RNN_WIDTH, D_MODEL), RNN_WIDTH ** -0.5),
        "w_out": nrm(ks[16], (DEPTH, D_MODEL, D_MODEL), D_MODEL ** -0.5),
        "final_g": 1.0 + nrm(ks[17], (D_MODEL,), 0.02),
    }


def reference(x, c, positions, w_ada, b_ada, norm_g, w_in, attn_sinks, conv_w, conv_b,
              rg_wa, rg_ba, rg_wx, rg_bx, rg_lambda, w_attn_proj, w_rnn_proj, w_out, final_g):
    B, S, _ = x.shape
    for l in range(DEPTH):
        mod = c @ w_ada[l] + b_ada[l]
        shift, scale, gate = jnp.split(mod, 3, axis=-1)
        h = rmsnorm(x, norm_g[l]) * (1.0 + scale[:, None, :]) + shift[:, None, :]
        proj = h @ w_in[l]
        q, k, v, g_attn, xr, g_rnn, m_attn, m_rnn = jnp.split(proj, SPLIT_POINTS, axis=-1)
        q = partial_rope(q.reshape(B, S, N_HEADS, HEAD_DIM), positions)
        k = partial_rope(k.reshape(B, S, N_KV_HEADS, HEAD_DIM), positions)
        v = v.reshape(B, S, N_KV_HEADS, HEAD_DIM)
        y_attn = sliding_window_attention_with_sinks(q, k, v, attn_sinks[l]) * jax.nn.silu(g_attn)
        xr = causal_depthwise_conv(xr, conv_w[l], conv_b[l])
        y_rnn = rg_lru(xr, positions, rg_wa[l], rg_ba[l], rg_wx[l], rg_bx[l], rg_lambda[l]) * jax.nn.silu(g_rnn)
        merged = (jax.nn.sigmoid(m_attn) * (y_attn @ w_attn_proj[l])
                  + jax.nn.sigmoid(m_rnn) * (y_rnn @ w_rnn_proj[l]))
        x = x + gate[:, None, :] * (merged @ w_out[l])
    return rmsnorm(x, final_g)
```

```python
import functools
import math

import numpy as np
import jax
import jax.numpy as jnp
from jax import lax
from jax.experimental import pallas as pl
from jax.experimental.pallas import tpu as pltpu

F32 = jnp.float32
BF16 = jnp.bfloat16

D_MODEL = 1024
HEAD_DIM = 64
N_HEADS = 16
N_KV = 4
KV_WIDTH = N_KV * HEAD_DIM
BLOCK = 128
ROT_DIM = 16
ROPE_THETA = 500000.0
RNN_BLOCK = 256
N_RNN_BLOCKS = 4
LRU_C = 8.0
CONV_WIDTH = 4
NORM_EPS = 1e-6
NEG = -1e30

LANES = 128
SUBLANES = 8
NG = D_MODEL // LANES

OFF_Q = 0
OFF_K = OFF_Q + D_MODEL
OFF_V = OFF_K + KV_WIDTH
OFF_GA = OFF_V + KV_WIDTH
OFF_XR = OFF_GA + D_MODEL
OFF_GR = OFF_XR + D_MODEL
OFF_MA = OFF_GR + D_MODEL
OFF_MR = OFF_MA + D_MODEL
IN_WIDTH = OFF_MR + D_MODEL

BT = 4
NC = 256
TAIL = SUBLANES
PITCH = BLOCK + TAIL
VMEM_LIMIT = 60 * 1024 * 1024

T_COS, T_SIN, T_RESET, T_ONE = 0, 8, 16, 17


def _expansion_matrix():
    e = np.zeros((LANES, 4 * LANES), np.float32)
    for l in range(LANES):
        d = l % HEAD_DIM
        if d < ROT_DIM:
            e[T_COS + d % 8, l] = 1.0
        else:
            e[T_ONE, l] = 1.0
        if d < 8:
            e[T_SIN + d, LANES + l] = -1.0
        elif d < ROT_DIM:
            e[T_SIN + d - 8, 2 * LANES + l] = 1.0
        e[T_RESET, 3 * LANES + l] = 1.0
    return np.concatenate([e, e, e], axis=0)


def _sigmoid(x):
    return 0.5 * jnp.tanh(0.5 * x) + 0.5


def _silu(x):
    return x * _sigmoid(x)


def _dot(a, b):
    return jnp.dot(a, b, preferred_element_type=F32)


def _rope(t, tab):
    c = tab[:, 0:LANES]
    s1 = tab[:, LANES:2 * LANES]
    s2 = tab[:, 2 * LANES:3 * LANES]
    return t * c + pltpu.roll(t, LANES - 8, 1) * s1 + pltpu.roll(t, 8, 1) * s2


def _mod_kernel(c_ref, w_ref, b_ref, o_ref):
    o_ref[...] = _dot(c_ref[...], w_ref[...]) + b_ref[...]


def _block_kernel(sinks_ref, x_ref, pos_ref, mod_ref, ng_ref, fg_ref, e3_ref,
                  win_ref, cw_ref, cb_ref, wa_ref, ba_ref, wx_ref, bx_ref, lam_ref,
                  wap_ref, wrp_ref, wout_ref,
                  out_ref,
                  h_scr, q_scr, k2_scr, v2_scr, tab_scr, mask_scr, sga_scr, ost_scr, ya_scr,
                  xe_scr, xc_scr, a_scr, b_scr, hst_scr, yr_scr, mg_scr):
    s_idx = pl.program_id(1)
    rows = BT * BLOCK

    @pl.when(s_idx == 0)
    def _init_state():
        zk = jnp.zeros((BT, N_KV, BLOCK, LANES), BF16)
        k2_scr[:, :, 0:BLOCK, :] = zk
        k2_scr[:, :, 2 * BLOCK:3 * BLOCK, :] = zk
        v2_scr[:, :, 0:BLOCK, :] = zk
        v2_scr[:, :, 2 * BLOCK:3 * BLOCK, :] = zk
        xe_scr[:, 0:TAIL, :] = jnp.zeros((BT, TAIL, D_MODEL), F32)
        hst_scr[...] = jnp.zeros((NG, BT, LANES), F32)

    qt = lax.broadcasted_iota(jnp.int32, (2 * BLOCK, 4 * BLOCK), 0) & (BLOCK - 1)
    kc = lax.broadcasted_iota(jnp.int32, (2 * BLOCK, 4 * BLOCK), 1) & (2 * BLOCK - 1)
    dist = kc - qt
    first_key = jnp.where(s_idx == 0, BLOCK, 0)
    valid = (dist >= 1) & (dist <= BLOCK) & (kc >= first_key)
    mask_scr[...] = jnp.where(valid, 0.0, NEG)

    r16 = lax.broadcasted_iota(jnp.int32, (2 * SUBLANES, LANES), 0)
    inv_freq = jnp.exp((r16 & 7).astype(F32) * (-math.log(ROPE_THETA) / 8.0))
    r8 = lax.broadcasted_iota(jnp.int32, (SUBLANES, LANES), 0)
    for b in range(BT):
        posr = pos_ref[0, b:b + 1, :]
        ang = inv_freq * posr.astype(F32)
        tab16 = jnp.where(r16 < 8, jnp.cos(ang), jnp.sin(ang))
        reset = jnp.where(posr == 0, 1.0, 0.0)
        extra = jnp.where(r8 == 0, reset, jnp.where(r8 == 1, 1.0, 0.0))
        tpad = jnp.concatenate(
            [tab16, extra, jnp.zeros((LANES - 3 * SUBLANES, LANES), F32)], axis=0)
        tt = tpad.T
        hi = tt.astype(BF16)
        r1 = tt - hi.astype(F32)
        mid = r1.astype(BF16)
        lo = (r1 - mid.astype(F32)).astype(BF16)
        tab_scr[b] = _dot(jnp.concatenate([hi, mid, lo], axis=1), e3_ref[...])

    ng = ng_ref[...]
    for b in range(BT):
        xb = x_ref[b]
        r = lax.rsqrt(jnp.mean(xb * xb, axis=-1, keepdims=True) + NORM_EPS)
        shift = mod_ref[0, b:b + 1, 0:D_MODEL]
        scale = mod_ref[0, b:b + 1, D_MODEL:2 * D_MODEL]
        hb = (xb * r) * ng * (1.0 + scale) + shift
        h_scr[b * BLOCK:(b + 1) * BLOCK, :] = hb.astype(BF16)

    h = h_scr[...]

    for c in range(D_MODEL // NC):
        qc = _dot(h, win_ref[:, OFF_Q + c * NC:OFF_Q + (c + 1) * NC])
        for b in range(BT):
            tab = tab_scr[b]
            for p in range(NC // LANES):
                t = qc[b * BLOCK:(b + 1) * BLOCK, p * LANES:(p + 1) * LANES]
                q_scr[b, c, p * BLOCK:(p + 1) * BLOCK, :] = (_rope(t, tab) * 0.125).astype(BF16)

    kv = _dot(h, win_ref[:, OFF_K:OFF_K + 2 * KV_WIDTH])
    low = lax.broadcasted_iota(jnp.int32, (BLOCK, LANES), 1) < HEAD_DIM
    for b in range(BT):
        tab = tab_scr[b]
        for u in range(KV_WIDTH // LANES):
            rs = slice(b * BLOCK, (b + 1) * BLOCK)
            kg = _rope(kv[rs, u * LANES:(u + 1) * LANES], tab)
            vg = kv[rs, KV_WIDTH + u * LANES:KV_WIDTH + (u + 1) * LANES]
            for src, src_rolled, dst in ((kg, pltpu.roll(kg, HEAD_DIM, 1), k2_scr),
                                         (vg, pltpu.roll(vg, HEAD_DIM, 1), v2_scr)):
                for hh in range(2):
                    j = 2 * u + hh
                    in_low = src if hh == 0 else src_rolled
                    in_high = src_rolled if hh == 0 else src
                    dst[b, j, BLOCK:2 * BLOCK, :] = jnp.where(low, in_low, 0.0).astype(BF16)
                    dst[b, j, 3 * BLOCK:4 * BLOCK, :] = jnp.where(low, 0.0, in_high).astype(BF16)

    for c in range(D_MODEL // NC):
        cs = slice(c * NC, (c + 1) * NC)
        sga_scr[:, cs] = _silu(_dot(h, win_ref[:, OFF_GA + c * NC:OFF_GA + (c + 1) * NC]))

    row_lo = lax.broadcasted_iota(jnp.int32, (2 * BLOCK, 1), 0) < BLOCK
    lane_lo = lax.broadcasted_iota(jnp.int32, (2 * BLOCK, LANES), 1) < HEAD_DIM
    ones_r = lax.broadcasted_iota(jnp.int32, (4 * BLOCK, LANES), 0) < 2 * BLOCK
    ones_l = lax.broadcasted_iota(jnp.int32, (4 * BLOCK, LANES), 1) < HEAD_DIM
    ones2 = jnp.where(ones_r == ones_l, 1.0, 0.0).astype(BF16)

    def attn_body(i, carry):
        b = i // N_KV
        j = i % N_KV
        q = q_scr[b, j]
        k2 = k2_scr[b, j]
        s = lax.dot_general(q, k2, (((1,), (1,)), ((), ())), preferred_element_type=F32)
        s = jnp.where(mask_scr[...] == 0.0, s, NEG)
        sink_a = jnp.where(row_lo, sinks_ref[4 * j], sinks_ref[4 * j + 2])
        sink_b = jnp.where(row_lo, sinks_ref[4 * j + 1], sinks_ref[4 * j + 3])
        s_a = s[:, 0:2 * BLOCK]
        s_b = s[:, 2 * BLOCK:4 * BLOCK]
        m_a = jnp.maximum(jnp.max(s_a, axis=1, keepdims=True), sink_a)
        m_b = jnp.maximum(jnp.max(s_b, axis=1, keepdims=True), sink_b)
        p = jnp.concatenate([jnp.exp(s_a - m_a), jnp.exp(s_b - m_b)], axis=1).astype(BF16)
        v3 = jnp.concatenate([v2_scr[b, j], ones2], axis=1)
        o = _dot(p, v3)
        e_sink = jnp.where(lane_lo, jnp.exp(sink_a - m_a), jnp.exp(sink_b - m_b))
        ost_scr[b, j] = o[:, 0:LANES] / (o[:, LANES:2 * LANES] + e_sink)
        return carry

    lax.fori_loop(0, BT * N_KV, attn_body, 0)

    for b in range(BT):
        rs = slice(b * BLOCK, (b + 1) * BLOCK)
        for g in range(NG):
            j, p = g // 2, g % 2
            gs = slice(g * LANES, (g + 1) * LANES)
            ya_scr[rs, gs] = (ost_scr[b, j, p * BLOCK:(p + 1) * BLOCK, :] * sga_scr[rs, gs]).astype(BF16)

    for c in range(D_MODEL // NC):
        cs = slice(c * NC, (c + 1) * NC)
        xr = _dot(h, win_ref[:, OFF_XR + c * NC:OFF_XR + (c + 1) * NC])
        for b in range(BT):
            xe_scr[b, TAIL:TAIL + BLOCK, cs] = xr[b * BLOCK:(b + 1) * BLOCK, :]
    cw = cw_ref[...]
    cb = cb_ref[...]
    for b in range(BT):
        acc = cw[0:1, :] * xe_scr[b, TAIL - 3:TAIL - 3 + BLOCK, :]
        for k in range(1, CONV_WIDTH):
            o0 = TAIL - (CONV_WIDTH - 1) + k
            acc = acc + cw[k:k + 1, :] * xe_scr[b, o0:o0 + BLOCK, :]
        xc_scr[b * BLOCK:(b + 1) * BLOCK, :] = acc + cb

    z = -lam_ref[...]
    sp = jnp.maximum(z, 0.0) + jnp.log1p(jnp.exp(-jnp.abs(z)))
    for blk in range(N_RNN_BLOCKS):
        bs = slice(blk * RNN_BLOCK, (blk + 1) * RNN_BLOCK)
        xcb = xc_scr[:, bs]
        xcb16 = xcb.astype(BF16)
        r = _sigmoid(_dot(xcb16, wa_ref[blk]) + ba_ref[:, bs])
        gi = _sigmoid(_dot(xcb16, wx_ref[blk]) + bx_ref[:, bs])
        log_a = (-LRU_C * r) * sp[:, bs]
        a = jnp.exp(log_a)
        mult = jnp.sqrt(1.0 - a * a)
        bx = gi * xcb
        for b in range(BT):
            rs = slice(b * BLOCK, (b + 1) * BLOCK)
            reset = tab_scr[b, :, 3 * LANES:4 * LANES] > 0.5
            for cc in range(RNN_BLOCK // LANES):
                ls = slice(cc * LANES, (cc + 1) * LANES)
                slab = blk * (RNN_BLOCK // LANES) + cc
                a_scr[slab, b * PITCH:b * PITCH + BLOCK, :] = jnp.where(reset, 0.0, a[rs, ls])
                b_scr[slab, b * PITCH:b * PITCH + BLOCK, :] = jnp.where(reset, 1.0, mult[rs, ls]) * bx[rs, ls]

    def scan_body(t, hs):
        new = []
        for c in range(NG):
            rows_t = pl.ds(t, BT, stride=PITCH)
            hc = a_scr[c, rows_t, :] * hs[c] + b_scr[c, rows_t, :]
            b_scr[c, rows_t, :] = hc
            new.append(hc)
        return tuple(new)

    hs = lax.fori_loop(0, BLOCK, scan_body, tuple(hst_scr[c] for c in range(NG)), unroll=8)
    for c in range(NG):
        hst_scr[c] = hs[c]

    for c in range(D_MODEL // NC):
        sg = _silu(_dot(h, win_ref[:, OFF_GR + c * NC:OFF_GR + (c + 1) * NC]))
        for b in range(BT):
            rs = slice(b * BLOCK, (b + 1) * BLOCK)
            for cc in range(NC // LANES):
                slab = c * (NC // LANES) + cc
                hv = b_scr[slab, b * PITCH:b * PITCH + BLOCK, :]
                yr_scr[rs, slab * LANES:(slab + 1) * LANES] = (
                    hv * sg[rs, cc * LANES:(cc + 1) * LANES]).astype(BF16)

    ya = ya_scr[...]
    yr = yr_scr[...]
    for c in range(D_MODEL // NC):
        cs = slice(c * NC, (c + 1) * NC)
        pa = _dot(ya, wap_ref[:, cs])
        pr = _dot(yr, wrp_ref[:, cs])
        ma = _dot(h, win_ref[:, OFF_MA + c * NC:OFF_MA + (c + 1) * NC])
        mr = _dot(h, win_ref[:, OFF_MR + c * NC:OFF_MR + (c + 1) * NC])
        mg_scr[:, cs] = (_sigmoid(ma) * pa + _sigmoid(mr) * pr).astype(BF16)

    o = _dot(mg_scr[...], wout_ref[...])
    fg = fg_ref[...]
    for b in range(BT):
        gate = mod_ref[0, b:b + 1, 2 * D_MODEL:3 * D_MODEL]
        y = x_ref[b] + gate * o[b * BLOCK:(b + 1) * BLOCK, :]
        r = lax.rsqrt(jnp.mean(y * y, axis=-1, keepdims=True) + NORM_EPS)
        out_ref[b] = (y * r) * fg

    k2_scr[:, :, 0:BLOCK, :] = k2_scr[:, :, BLOCK:2 * BLOCK, :]
    k2_scr[:, :, 2 * BLOCK:3 * BLOCK, :] = k2_scr[:, :, 3 * BLOCK:4 * BLOCK, :]
    v2_scr[:, :, 0:BLOCK, :] = v2_scr[:, :, BLOCK:2 * BLOCK, :]
    v2_scr[:, :, 2 * BLOCK:3 * BLOCK, :] = v2_scr[:, :, 3 * BLOCK:4 * BLOCK, :]
    xe_scr[:, 0:TAIL, :] = xe_scr[:, BLOCK:BLOCK + TAIL, :]


def _const_spec(shape):
    zeros = (0,) * len(shape)
    return pl.BlockSpec(shape, lambda g, s: zeros, pipeline_mode=pl.Buffered(1))


def kernel(x, c, positions, w_ada, b_ada, norm_g, w_in, attn_sinks, conv_w, conv_b, rg_wa, rg_ba, rg_wx, rg_bx, rg_lambda, w_attn_proj, w_rnn_proj, w_out, final_g):
    B, S, D = x.shape
    assert (D, S % BLOCK, B % BT) == (D_MODEL, 0, 0)
    assert w_in.shape == (1, D_MODEL, IN_WIDTH)
    n_groups, n_steps, rows = B // BT, S // BLOCK, BT * BLOCK

    mod = pl.pallas_call(
        _mod_kernel,
        grid=(3,),
        in_specs=[pl.BlockSpec((B, D), lambda i: (0, 0)),
                  pl.BlockSpec((D, D), lambda i: (0, i)),
                  pl.BlockSpec((1, D), lambda i: (0, i))],
        out_specs=pl.BlockSpec((B, D), lambda i: (0, i)),
        out_shape=jax.ShapeDtypeStruct((B, 3 * D), F32),
        name="adaln_mod",
    )(c, w_ada[0], b_ada)

    e3 = jnp.asarray(_expansion_matrix(), BF16)
    row = lambda v: v.reshape(1, -1)
    operands = (
        attn_sinks[0],
        x,
        positions.reshape(n_groups, BT, S),
        mod.reshape(n_groups, BT, 3 * D),
        norm_g, row(final_g), e3,
        w_in[0].astype(BF16), conv_w[0], conv_b, rg_wa[0].astype(BF16), rg_ba,
        rg_wx[0].astype(BF16), rg_bx, rg_lambda,
        w_attn_proj[0].astype(BF16), w_rnn_proj[0].astype(BF16), w_out[0].astype(BF16),
    )
    in_specs = [
        pl.BlockSpec(memory_space=pltpu.SMEM),
        pl.BlockSpec((BT, BLOCK, D), lambda g, s: (g, s, 0)),
        pl.BlockSpec((1, BT, BLOCK), lambda g, s: (g, 0, s)),
        pl.BlockSpec((1, BT, 3 * D), lambda g, s: (g, 0, 0)),
    ] + [_const_spec(op.shape) for op in operands[4:]]

    scratch = [
        pltpu.VMEM((rows, D), BF16),
        pltpu.VMEM((BT, N_KV, 2 * BLOCK, LANES), BF16),
        pltpu.VMEM((BT, N_KV, 4 * BLOCK, LANES), BF16),
        pltpu.VMEM((BT, N_KV, 4 * BLOCK, LANES), BF16),
        pltpu.VMEM((BT, BLOCK, 4 * LANES), F32),
        pltpu.VMEM((2 * BLOCK, 4 * BLOCK), F32),
        pltpu.VMEM((rows, D), F32),
        pltpu.VMEM((BT, N_KV, 2 * BLOCK, LANES), F32),
        pltpu.VMEM((rows, D), BF16),
        pltpu.VMEM((BT, PITCH, D), F32),
        pltpu.VMEM((rows, D), F32),
        pltpu.VMEM((NG, BT * PITCH, LANES), F32),
        pltpu.VMEM((NG, BT * PITCH, LANES), F32),
        pltpu.VMEM((NG, BT, LANES), F32),
        pltpu.VMEM((rows, D), BF16),
        pltpu.VMEM((rows, D), BF16),
    ]

    out = pl.pallas_call(
        _block_kernel,
        grid=(n_groups, n_steps),
        in_specs=in_specs,
        out_specs=pl.BlockSpec((BT, BLOCK, D), lambda g, s: (g, s, 0)),
        out_shape=jax.ShapeDtypeStruct((B, S, D), F32),
        scratch_shapes=scratch,
        compiler_params=pltpu.CompilerParams(
            dimension_semantics=("arbitrary", "arbitrary"),
            vmem_limit_bytes=VMEM_LIMIT),
        name="hybrid_block",
    )(*operands)
    return out
```

```python
import functools
import math

import numpy as np
import jax
import jax.numpy as jnp
from jax import lax
from jax.experimental import pallas as pl
from jax.experimental.pallas import tpu as pltpu

F32 = jnp.float32
BF16 = jnp.bfloat16

D_MODEL = 1024
HEAD_DIM = 64
N_HEADS = 16
N_KV = 4
KV_WIDTH = N_KV * HEAD_DIM
BLOCK = 128
ROT_DIM = 16
ROPE_THETA = 500000.0
RNN_BLOCK = 256
N_RNN_BLOCKS = 4
LRU_C = 8.0
CONV_WIDTH = 4
NORM_EPS = 1e-6
NEG = -1e30

LANES = 128
SUBLANES = 8
NG = D_MODEL // LANES

OFF_Q = 0
OFF_K = OFF_Q + D_MODEL
OFF_V = OFF_K + KV_WIDTH
OFF_GA = OFF_V + KV_WIDTH
OFF_XR = OFF_GA + D_MODEL
OFF_GR = OFF_XR + D_MODEL
OFF_MA = OFF_GR + D_MODEL
OFF_MR = OFF_MA + D_MODEL
IN_WIDTH = OFF_MR + D_MODEL

BT = 4
NC = 256
TAIL = SUBLANES
PITCH = BLOCK + TAIL
VMEM_LIMIT = 60 * 1024 * 1024

T_COS, T_SIN, T_RESET, T_ONE = 0, 8, 16, 17


def _expansion_matrix():
    e = np.zeros((LANES, 4 * LANES), np.float32)
    for l in range(LANES):
        d = l % HEAD_DIM
        if d < ROT_DIM:
            e[T_COS + d % 8, l] = 1.0
        else:
            e[T_ONE, l] = 1.0
        if d < 8:
            e[T_SIN + d, LANES + l] = -1.0
        elif d < ROT_DIM:
            e[T_SIN + d - 8, 2 * LANES + l] = 1.0
        e[T_RESET, 3 * LANES + l] = 1.0
    return np.concatenate([e, e, e], axis=0)


def _sigmoid(x):
    return 0.5 * jnp.tanh(0.5 * x) + 0.5


def _silu(x):
    return x * _sigmoid(x)


def _dot(a, b):
    return jnp.dot(a, b, preferred_element_type=F32)


def _rope(t, tab):
    c = tab[:, 0:LANES]
    s1 = tab[:, LANES:2 * LANES]
    s2 = tab[:, 2 * LANES:3 * LANES]
    return t * c + pltpu.roll(t, LANES - 8, 1) * s1 + pltpu.roll(t, 8, 1) * s2


def _mod_kernel(c_ref, w_ref, b_ref, o_ref):
    o_ref[...] = _dot(c_ref[...], w_ref[...]) + b_ref[...]


def _block_kernel(sinks_ref, x_ref, pos_ref, mod_ref, ng_ref, fg_ref, e3_ref,
                  win_ref, cw_ref, cb_ref, wa_ref, ba_ref, wx_ref, bx_ref, lam_ref,
                  wap_ref, wrp_ref, wout_ref,
                  out_ref,
                  h_scr, q_scr, k2_scr, v2_scr, tab_scr, mask_scr, sga_scr, p_scr, ya_scr,
                  xe_scr, xc_scr, a_scr, b_scr, hst_scr, yr_scr, mg_scr):
    s_idx = pl.program_id(1)
    rows = BT * BLOCK

    @pl.when(s_idx == 0)
    def _init_state():
        zk = jnp.zeros((BT, N_KV, BLOCK, LANES), BF16)
        k2_scr[:, :, 0:BLOCK, :] = zk
        k2_scr[:, :, 2 * BLOCK:3 * BLOCK, :] = zk
        v2_scr[:, :, 0:BLOCK, :] = zk
        v2_scr[:, :, 2 * BLOCK:3 * BLOCK, :] = zk
        xe_scr[:, 0:TAIL, :] = jnp.zeros((BT, TAIL, D_MODEL), F32)
        hst_scr[...] = jnp.zeros((NG, BT, LANES), F32)

    qt = lax.broadcasted_iota(jnp.int32, (2 * BLOCK, 4 * BLOCK), 0) & (BLOCK - 1)
    kc = lax.broadcasted_iota(jnp.int32, (2 * BLOCK, 4 * BLOCK), 1) & (2 * BLOCK - 1)
    dist = kc - qt
    first_key = jnp.where(s_idx == 0, BLOCK, 0)
    valid = (dist >= 1) & (dist <= BLOCK) & (kc >= first_key)
    mask_scr[...] = jnp.where(valid, 0.0, NEG)

    r16 = lax.broadcasted_iota(jnp.int32, (2 * SUBLANES, LANES), 0)
    inv_freq = jnp.exp((r16 & 7).astype(F32) * (-math.log(ROPE_THETA) / 8.0))
    r8 = lax.broadcasted_iota(jnp.int32, (SUBLANES, LANES), 0)
    for b in range(BT):
        posr = pos_ref[0, b:b + 1, :]
        ang = inv_freq * posr.astype(F32)
        tab16 = jnp.where(r16 < 8, jnp.cos(ang), jnp.sin(ang))
        reset = jnp.where(posr == 0, 1.0, 0.0)
        extra = jnp.where(r8 == 0, reset, jnp.where(r8 == 1, 1.0, 0.0))
        tpad = jnp.concatenate(
            [tab16, extra, jnp.zeros((LANES - 3 * SUBLANES, LANES), F32)], axis=0)
        tt = tpad.T
        hi = tt.astype(BF16)
        r1 = tt - hi.astype(F32)
        mid = r1.astype(BF16)
        lo = (r1 - mid.astype(F32)).astype(BF16)
        tab_scr[b] = _dot(jnp.concatenate([hi, mid, lo], axis=1), e3_ref[...])

    ng = ng_ref[...]
    for b in range(BT):
        xb = x_ref[b]
        r = lax.rsqrt(jnp.mean(xb * xb, axis=-1, keepdims=True) + NORM_EPS)
        shift = mod_ref[0, b:b + 1, 0:D_MODEL]
        scale = mod_ref[0, b:b + 1, D_MODEL:2 * D_MODEL]
        hb = (xb * r) * ng * (1.0 + scale) + shift
        h_scr[b * BLOCK:(b + 1) * BLOCK, :] = hb.astype(BF16)

    h = h_scr[...]

    for c in range(D_MODEL // NC):
        qc = _dot(h, win_ref[:, OFF_Q + c * NC:OFF_Q + (c + 1) * NC])
        for b in range(BT):
            tab = tab_scr[b]
            for p in range(NC // LANES):
                t = qc[b * BLOCK:(b + 1) * BLOCK, p * LANES:(p + 1) * LANES]
                q_scr[b, c, p * BLOCK:(p + 1) * BLOCK, :] = (_rope(t, tab) * 0.125).astype(BF16)

    kv = _dot(h, win_ref[:, OFF_K:OFF_K + 2 * KV_WIDTH])
    low = lax.broadcasted_iota(jnp.int32, (BLOCK, LANES), 1) < HEAD_DIM
    for b in range(BT):
        tab = tab_scr[b]
        for u in range(KV_WIDTH // LANES):
            rs = slice(b * BLOCK, (b + 1) * BLOCK)
            kg = _rope(kv[rs, u * LANES:(u + 1) * LANES], tab)
            vg = kv[rs, KV_WIDTH + u * LANES:KV_WIDTH + (u + 1) * LANES]
            for src, src_rolled, dst in ((kg, pltpu.roll(kg, HEAD_DIM, 1), k2_scr),
                                         (vg, pltpu.roll(vg, HEAD_DIM, 1), v2_scr)):
                for hh in range(2):
                    j = 2 * u + hh
                    in_low = src if hh == 0 else src_rolled
                    in_high = src_rolled if hh == 0 else src
                    dst[b, j, BLOCK:2 * BLOCK, :] = jnp.where(low, in_low, 0.0).astype(BF16)
                    dst[b, j, 3 * BLOCK:4 * BLOCK, :] = jnp.where(low, 0.0, in_high).astype(BF16)

    row_lo = lax.broadcasted_iota(jnp.int32, (2 * BLOCK, 1), 0) < BLOCK
    col0 = lax.broadcasted_iota(jnp.int32, (2 * BLOCK, LANES), 1) == 0
    for b in range(BT):
        for j in range(N_KV):
            s = lax.dot_general(q_scr[b, j], k2_scr[b, j], (((1,), (1,)), ((), ())),
                                preferred_element_type=F32)
            s = jnp.where(mask_scr[...] == 0.0, s, NEG)
            sink_a = jnp.where(row_lo, sinks_ref[4 * j], sinks_ref[4 * j + 2])
            sink_b = jnp.where(row_lo, sinks_ref[4 * j + 1], sinks_ref[4 * j + 3])
            s_a = jnp.concatenate([jnp.where(col0, sink_a, s[:, 0:LANES]), s[:, LANES:2 * LANES]], axis=1)
            s_b = jnp.concatenate([jnp.where(col0, sink_b, s[:, 2 * LANES:3 * LANES]), s[:, 3 * LANES:]], axis=1)
            m_a = jnp.max(s_a, axis=1, keepdims=True)
            m_b = jnp.max(s_b, axis=1, keepdims=True)
            p_scr[b, j] = jnp.concatenate([jnp.exp(s_a - m_a), jnp.exp(s_b - m_b)], axis=1).astype(BF16)

    for c in range(D_MODEL // NC):
        cs = slice(c * NC, (c + 1) * NC)
        sga_scr[:, cs] = _silu(_dot(h, win_ref[:, OFF_GA + c * NC:OFF_GA + (c + 1) * NC]))

    ones_r = lax.broadcasted_iota(jnp.int32, (4 * BLOCK, LANES), 0) < 2 * BLOCK
    ones_l = lax.broadcasted_iota(jnp.int32, (4 * BLOCK, LANES), 1) < HEAD_DIM
    ones2 = jnp.where(ones_r == ones_l, 1.0, 0.0).astype(BF16)
    for b in range(BT):
        rs = slice(b * BLOCK, (b + 1) * BLOCK)
        for j in range(N_KV):
            v3 = jnp.concatenate([v2_scr[b, j], ones2], axis=1)
            o = _dot(p_scr[b, j], v3)
            res = o[:, 0:LANES] / o[:, LANES:2 * LANES]
            for p in range(2):
                gs = slice((2 * j + p) * LANES, (2 * j + p + 1) * LANES)
                ya_scr[rs, gs] = (res[p * BLOCK:(p + 1) * BLOCK, :] * sga_scr[rs, gs]).astype(BF16)

    for c in range(D_MODEL // NC):
        cs = slice(c * NC, (c + 1) * NC)
        xr = _dot(h, win_ref[:, OFF_XR + c * NC:OFF_XR + (c + 1) * NC])
        for b in range(BT):
            xe_scr[b, TAIL:TAIL + BLOCK, cs] = xr[b * BLOCK:(b + 1) * BLOCK, :]
    cw = cw_ref[...]
    cb = cb_ref[...]
    for b in range(BT):
        acc = cw[0:1, :] * xe_scr[b, TAIL - 3:TAIL - 3 + BLOCK, :]
        for k in range(1, CONV_WIDTH):
            o0 = TAIL - (CONV_WIDTH - 1) + k
            acc = acc + cw[k:k + 1, :] * xe_scr[b, o0:o0 + BLOCK, :]
        xc_scr[b * BLOCK:(b + 1) * BLOCK, :] = acc + cb

    z = -lam_ref[...]
    sp = jnp.maximum(z, 0.0) + jnp.log1p(jnp.exp(-jnp.abs(z)))
    for blk in range(N_RNN_BLOCKS):
        bs = slice(blk * RNN_BLOCK, (blk + 1) * RNN_BLOCK)
        xcb = xc_scr[:, bs]
        xcb16 = xcb.astype(BF16)
        r = _sigmoid(_dot(xcb16, wa_ref[blk]) + ba_ref[:, bs])
        gi = _sigmoid(_dot(xcb16, wx_ref[blk]) + bx_ref[:, bs])
        log_a = (-LRU_C * r) * sp[:, bs]
        a = jnp.exp(log_a)
        mult = jnp.sqrt(1.0 - a * a)
        bx = gi * xcb
        for b in range(BT):
            rs = slice(b * BLOCK, (b + 1) * BLOCK)
            reset = tab_scr[b, :, 3 * LANES:4 * LANES] > 0.5
            for cc in range(RNN_BLOCK // LANES):
                ls = slice(cc * LANES, (cc + 1) * LANES)
                slab = blk * (RNN_BLOCK // LANES) + cc
                a_scr[slab, b * PITCH:b * PITCH + BLOCK, :] = jnp.where(reset, 0.0, a[rs, ls])
                b_scr[slab, b * PITCH:b * PITCH + BLOCK, :] = jnp.where(reset, 1.0, mult[rs, ls]) * bx[rs, ls]

    def scan_body(t, hs):
        new = []
        for c in range(NG):
            rows_t = pl.ds(t, BT, stride=PITCH)
            hc = a_scr[c, rows_t, :] * hs[c] + b_scr[c, rows_t, :]
            b_scr[c, rows_t, :] = hc
            new.append(hc)
        return tuple(new)

    hs = lax.fori_loop(0, BLOCK, scan_body, tuple(hst_scr[c] for c in range(NG)), unroll=8)
    for c in range(NG):
        hst_scr[c] = hs[c]

    for c in range(D_MODEL // NC):
        sg = _silu(_dot(h, win_ref[:, OFF_GR + c * NC:OFF_GR + (c + 1) * NC]))
        for b in range(BT):
            rs = slice(b * BLOCK, (b + 1) * BLOCK)
            for cc in range(NC // LANES):
                slab = c * (NC // LANES) + cc
                hv = b_scr[slab, b * PITCH:b * PITCH + BLOCK, :]
                yr_scr[rs, slab * LANES:(slab + 1) * LANES] = (
                    hv * sg[rs, cc * LANES:(cc + 1) * LANES]).astype(BF16)

    ya = ya_scr[...]
    yr = yr_scr[...]
    for c in range(D_MODEL // NC):
        cs = slice(c * NC, (c + 1) * NC)
        pa = _dot(ya, wap_ref[:, cs])
        pr = _dot(yr, wrp_ref[:, cs])
        ma = _dot(h, win_ref[:, OFF_MA + c * NC:OFF_MA + (c + 1) * NC])
        mr = _dot(h, win_ref[:, OFF_MR + c * NC:OFF_MR + (c + 1) * NC])
        mg_scr[:, cs] = (_sigmoid(ma) * pa + _sigmoid(mr) * pr).astype(BF16)

    o = _dot(mg_scr[...], wout_ref[...])
    fg = fg_ref[...]
    for b in range(BT):
        gate = mod_ref[0, b:b + 1, 2 * D_MODEL:3 * D_MODEL]
        y = x_ref[b] + gate * o[b * BLOCK:(b + 1) * BLOCK, :]
        r = lax.rsqrt(jnp.mean(y * y, axis=-1, keepdims=True) + NORM_EPS)
        out_ref[b] = (y * r) * fg

    k2_scr[:, :, 0:BLOCK, :] = k2_scr[:, :, BLOCK:2 * BLOCK, :]
    k2_scr[:, :, 2 * BLOCK:3 * BLOCK, :] = k2_scr[:, :, 3 * BLOCK:4 * BLOCK, :]
    not_row0 = lax.broadcasted_iota(jnp.int32, (BT, N_KV, BLOCK, LANES), 2) > 0
    zero = jnp.zeros((), BF16)
    v2_scr[:, :, 0:BLOCK, :] = jnp.where(not_row0, v2_scr[:, :, BLOCK:2 * BLOCK, :], zero)
    v2_scr[:, :, 2 * BLOCK:3 * BLOCK, :] = jnp.where(not_row0, v2_scr[:, :, 3 * BLOCK:4 * BLOCK, :], zero)
    xe_scr[:, 0:TAIL, :] = xe_scr[:, BLOCK:BLOCK + TAIL, :]


def _const_spec(shape):
    zeros = (0,) * len(shape)
    return pl.BlockSpec(shape, lambda g, s: zeros, pipeline_mode=pl.Buffered(1))


def kernel(x, c, positions, w_ada, b_ada, norm_g, w_in, attn_sinks, conv_w, conv_b, rg_wa, rg_ba, rg_wx, rg_bx, rg_lambda, w_attn_proj, w_rnn_proj, w_out, final_g):
    B, S, D = x.shape
    assert (D, S % BLOCK, B % BT) == (D_MODEL, 0, 0)
    assert w_in.shape == (1, D_MODEL, IN_WIDTH)
    n_groups, n_steps, rows = B // BT, S // BLOCK, BT * BLOCK

    mod = pl.pallas_call(
        _mod_kernel,
        grid=(3,),
        in_specs=[pl.BlockSpec((B, D), lambda i: (0, 0)),
                  pl.BlockSpec((D, D), lambda i: (0, i)),
                  pl.BlockSpec((1, D), lambda i: (0, i))],
        out_specs=pl.BlockSpec((B, D), lambda i: (0, i)),
        out_shape=jax.ShapeDtypeStruct((B, 3 * D), F32),
        name="adaln_mod",
    )(c, w_ada[0], b_ada)

    e3 = jnp.asarray(_expansion_matrix(), BF16)
    row = lambda v: v.reshape(1, -1)
    operands = (
        attn_sinks[0],
        x,
        positions.reshape(n_groups, BT, S),
        mod.reshape(n_groups, BT, 3 * D),
        norm_g, row(final_g), e3,
        w_in[0].astype(BF16), conv_w[0], conv_b, rg_wa[0].astype(BF16), rg_ba,
        rg_wx[0].astype(BF16), rg_bx, rg_lambda,
        w_attn_proj[0].astype(BF16), w_rnn_proj[0].astype(BF16), w_out[0].astype(BF16),
    )
    in_specs = [
        pl.BlockSpec(memory_space=pltpu.SMEM),
        pl.BlockSpec((BT, BLOCK, D), lambda g, s: (g, s, 0)),
        pl.BlockSpec((1, BT, BLOCK), lambda g, s: (g, 0, s)),
        pl.BlockSpec((1, BT, 3 * D), lambda g, s: (g, 0, 0)),
    ] + [_const_spec(op.shape) for op in operands[4:]]

    scratch = [
        pltpu.VMEM((rows, D), BF16),
        pltpu.VMEM((BT, N_KV, 2 * BLOCK, LANES), BF16),
        pltpu.VMEM((BT, N_KV, 4 * BLOCK, LANES), BF16),
        pltpu.VMEM((BT, N_KV, 4 * BLOCK, LANES), BF16),
        pltpu.VMEM((BT, BLOCK, 4 * LANES), F32),
        pltpu.VMEM((2 * BLOCK, 4 * BLOCK), F32),
        pltpu.VMEM((rows, D), F32),
        pltpu.VMEM((BT, N_KV, 2 * BLOCK, 4 * BLOCK), BF16),
        pltpu.VMEM((rows, D), BF16),
        pltpu.VMEM((BT, PITCH, D), F32),
        pltpu.VMEM((rows, D), F32),
        pltpu.VMEM((NG, BT * PITCH, LANES), F32),
        pltpu.VMEM((NG, BT * PITCH, LANES), F32),
        pltpu.VMEM((NG, BT, LANES), F32),
        pltpu.VMEM((rows, D), BF16),
        pltpu.VMEM((rows, D), BF16),
    ]

    out = pl.pallas_call(
        _block_kernel,
        grid=(n_groups, n_steps),
        in_specs=in_specs,
        out_specs=pl.BlockSpec((BT, BLOCK, D), lambda g, s: (g, s, 0)),
        out_shape=jax.ShapeDtypeStruct((B, S, D), F32),
        scratch_shapes=scratch,
        compiler_params=pltpu.CompilerParams(
            dimension_semantics=("arbitrary", "arbitrary"),
            vmem_limit_bytes=VMEM_LIMIT),
        name="hybrid_block",
    )(*operands)
    return out
```

```python
import math

import numpy as np
import jax
import jax.numpy as jnp
from jax import lax
from jax.experimental import pallas as pl
from jax.experimental.pallas import tpu as pltpu

F32 = jnp.float32
BF16 = jnp.bfloat16

D_MODEL = 1024
HEAD_DIM = 64
N_HEADS = 16
N_KV = 4
KV_WIDTH = N_KV * HEAD_DIM
BLOCK = 128
ROT_DIM = 16
ROPE_THETA = 500000.0
RNN_BLOCK = 256
N_RNN_BLOCKS = 4
LRU_C = 8.0
CONV_WIDTH = 4
NORM_EPS = 1e-6
NEG = -1e30

LANES = 128
SUBLANES = 8
NG = D_MODEL // LANES

OFF_Q = 0
OFF_K = OFF_Q + D_MODEL
OFF_V = OFF_K + KV_WIDTH
OFF_GA = OFF_V + KV_WIDTH
OFF_XR = OFF_GA + D_MODEL
OFF_GR = OFF_XR + D_MODEL
OFF_MA = OFF_GR + D_MODEL
OFF_MR = OFF_MA + D_MODEL
IN_WIDTH = OFF_MR + D_MODEL

BT = 4
NC = 256
TAIL = SUBLANES
PITCH = BLOCK + TAIL
VMEM_LIMIT = 62 * 1024 * 1024

T_COS, T_SIN, T_RESET, T_ONE = 0, 8, 16, 17


def _expansion_matrix():
    e = np.zeros((LANES, 4 * LANES), np.float32)
    for l in range(LANES):
        d = l % HEAD_DIM
        if d < ROT_DIM:
            e[T_COS + d % 8, l] = 1.0
        else:
            e[T_ONE, l] = 1.0
        if d < 8:
            e[T_SIN + d, LANES + l] = -1.0
        elif d < ROT_DIM:
            e[T_SIN + d - 8, 2 * LANES + l] = 1.0
        e[T_RESET, 3 * LANES + l] = 1.0
    return np.concatenate([e, e, e], axis=0)


def _sigmoid(x):
    return 0.5 * jnp.tanh(0.5 * x) + 0.5


def _silu(x):
    u = 0.5 * x
    return u * jnp.tanh(u) + u


def _dot(a, b):
    return jnp.dot(a, b, preferred_element_type=F32)


def _rope(t, tab):
    c = tab[:, 0:LANES]
    s1 = tab[:, LANES:2 * LANES]
    s2 = tab[:, 2 * LANES:3 * LANES]
    return t * c + pltpu.roll(t, LANES - 8, 1) * s1 + pltpu.roll(t, 8, 1) * s2


def _mod_kernel(c_ref, w_ref, b_ref, o_ref):
    o_ref[...] = _dot(c_ref[...], w_ref[...]) + b_ref[...]


def _block_kernel(sinks_ref, x_ref, pos_ref, mod_ref, ng_ref, fg_ref, e3_ref,
                  win_ref, cw_ref, cb_ref, wa_ref, ba_ref, wx_ref, bx_ref, lam_ref,
                  wap_ref, wrp_ref, wout_ref,
                  out_ref,
                  h_scr, q_scr, k2_scr, v2_scr, tab_scr, mask_scr, sga_scr, sma_scr, smr_scr, sgr_scr,
                  p_scr, ya_scr, xe_scr, xc_scr, a_scr, b_scr, hst_scr, yr_scr, m1_scr, mg_scr):
    s_idx = pl.program_id(1)
    n_chunks = D_MODEL // NC

    @pl.when(s_idx == 0)
    def _init_state():
        zk = jnp.zeros((BT, N_KV, BLOCK, LANES), BF16)
        k2_scr[:, :, 0:BLOCK, :] = zk
        k2_scr[:, :, 2 * BLOCK:3 * BLOCK, :] = zk
        v2_scr[:, :, 0:BLOCK, :] = zk
        v2_scr[:, :, 2 * BLOCK:3 * BLOCK, :] = zk
        xe_scr[:, 0:TAIL, :] = jnp.zeros((BT, TAIL, D_MODEL), F32)
        hst_scr[...] = jnp.zeros((NG, BT, LANES), F32)

    qt = lax.broadcasted_iota(jnp.int32, (2 * BLOCK, 4 * BLOCK), 0) & (BLOCK - 1)
    kc = lax.broadcasted_iota(jnp.int32, (2 * BLOCK, 4 * BLOCK), 1) & (2 * BLOCK - 1)
    dist = kc - qt
    first_key = jnp.where(s_idx == 0, BLOCK, 0)
    valid = (dist >= 1) & (dist <= BLOCK) & (kc >= first_key)
    mask_scr[...] = jnp.where(valid, 0.0, NEG)

    r16 = lax.broadcasted_iota(jnp.int32, (2 * SUBLANES, LANES), 0)
    inv_freq = jnp.exp((r16 & 7).astype(F32) * (-math.log(ROPE_THETA) / 8.0))
    r8 = lax.broadcasted_iota(jnp.int32, (SUBLANES, LANES), 0)
    for b in range(BT):
        posr = pos_ref[0, b:b + 1, :]
        ang = inv_freq * posr.astype(F32)
        tab16 = jnp.where(r16 < 8, jnp.cos(ang), jnp.sin(ang))
        reset = jnp.where(posr == 0, 1.0, 0.0)
        extra = jnp.where(r8 == 0, reset, jnp.where(r8 == 1, 1.0, 0.0))
        tpad = jnp.concatenate(
            [tab16, extra, jnp.zeros((LANES - 3 * SUBLANES, LANES), F32)], axis=0)
        tt = tpad.T
        hi = tt.astype(BF16)
        r1 = tt - hi.astype(F32)
        mid = r1.astype(BF16)
        lo = (r1 - mid.astype(F32)).astype(BF16)
        tab_scr[b] = _dot(jnp.concatenate([hi, mid, lo], axis=1), e3_ref[...])

    ng = ng_ref[...]
    for b in range(BT):
        xb = x_ref[b]
        r = lax.rsqrt(jnp.mean(xb * xb, axis=-1, keepdims=True) + NORM_EPS)
        shift = mod_ref[0, b:b + 1, 0:D_MODEL]
        scale = mod_ref[0, b:b + 1, D_MODEL:2 * D_MODEL]
        hb = (xb * r) * (ng * (1.0 + scale)) + shift
        h_scr[b * BLOCK:(b + 1) * BLOCK, :] = hb.astype(BF16)

    h = h_scr[...]

    for c in range(n_chunks):
        qc = _dot(h, win_ref[:, OFF_Q + c * NC:OFF_Q + (c + 1) * NC])
        for b in range(BT):
            tab = tab_scr[b]
            for p in range(NC // LANES):
                t = qc[b * BLOCK:(b + 1) * BLOCK, p * LANES:(p + 1) * LANES]
                q_scr[b, c, p * BLOCK:(p + 1) * BLOCK, :] = (_rope(t, tab) * 0.125).astype(BF16)

    kv = _dot(h, win_ref[:, OFF_K:OFF_K + 2 * KV_WIDTH])
    low = lax.broadcasted_iota(jnp.int32, (BLOCK, LANES), 1) < HEAD_DIM
    for b in range(BT):
        tab = tab_scr[b]
        for u in range(KV_WIDTH // LANES):
            rs = slice(b * BLOCK, (b + 1) * BLOCK)
            kg = _rope(kv[rs, u * LANES:(u + 1) * LANES], tab)
            vg = kv[rs, KV_WIDTH + u * LANES:KV_WIDTH + (u + 1) * LANES]
            for src, src_rolled, dst in ((kg, pltpu.roll(kg, HEAD_DIM, 1), k2_scr),
                                         (vg, pltpu.roll(vg, HEAD_DIM, 1), v2_scr)):
                for hh in range(2):
                    j = 2 * u + hh
                    in_low = src if hh == 0 else src_rolled
                    in_high = src_rolled if hh == 0 else src
                    dst[b, j, BLOCK:2 * BLOCK, :] = jnp.where(low, in_low, 0.0).astype(BF16)
                    dst[b, j, 3 * BLOCK:4 * BLOCK, :] = jnp.where(low, 0.0, in_high).astype(BF16)

    for c in range(n_chunks):
        cs = slice(c * NC, (c + 1) * NC)
        xr = _dot(h, win_ref[:, OFF_XR + c * NC:OFF_XR + (c + 1) * NC])
        for b in range(BT):
            xe_scr[b, TAIL:TAIL + BLOCK, cs] = xr[b * BLOCK:(b + 1) * BLOCK, :]

    row_lo = lax.broadcasted_iota(jnp.int32, (2 * BLOCK, 1), 0) < BLOCK
    col0 = lax.broadcasted_iota(jnp.int32, (2 * BLOCK, LANES), 1) == 0

    def attn_scores(b, j):
        s = lax.dot_general(q_scr[b, j], k2_scr[b, j], (((1,), (1,)), ((), ())),
                            preferred_element_type=F32)
        s = jnp.where(mask_scr[...] == 0.0, s, NEG)
        sink_a = jnp.where(row_lo, sinks_ref[4 * j], sinks_ref[4 * j + 2])
        sink_b = jnp.where(row_lo, sinks_ref[4 * j + 1], sinks_ref[4 * j + 3])
        s_a = jnp.concatenate([jnp.where(col0, sink_a, s[:, 0:LANES]), s[:, LANES:2 * LANES]], axis=1)
        s_b = jnp.concatenate([jnp.where(col0, sink_b, s[:, 2 * LANES:3 * LANES]), s[:, 3 * LANES:]], axis=1)
        m_a = jnp.max(s_a, axis=1, keepdims=True)
        m_b = jnp.max(s_b, axis=1, keepdims=True)
        p_scr[b, j] = jnp.concatenate([jnp.exp(s_a - m_a), jnp.exp(s_b - m_b)], axis=1).astype(BF16)

    gate_specs = ((OFF_GA, _silu, sga_scr), (OFF_MA, _sigmoid, sma_scr),
                  (OFF_MR, _sigmoid, smr_scr), (OFF_GR, _silu, sgr_scr))

    def gate_chunk(k):
        (off, act, dst), c = gate_specs[k // n_chunks], k % n_chunks
        dst[:, c * NC:(c + 1) * NC] = act(_dot(h, win_ref[:, off + c * NC:off + (c + 1) * NC])).astype(BF16)

    for b in range(BT):
        for j in range(N_KV):
            attn_scores(b, j)
            if j % 2 == 1:
                gate_chunk((b * N_KV + j) // 2)

    cw = cw_ref[...]
    cb = cb_ref[...]
    ones_r = lax.broadcasted_iota(jnp.int32, (4 * BLOCK, LANES), 0) < 2 * BLOCK
    ones_l = lax.broadcasted_iota(jnp.int32, (4 * BLOCK, LANES), 1) < HEAD_DIM
    ones2 = jnp.where(ones_r == ones_l, 1.0, 0.0).astype(BF16)
    for b in range(BT):
        rs = slice(b * BLOCK, (b + 1) * BLOCK)
        xe = xe_scr[b]
        acc = cw[CONV_WIDTH - 1:CONV_WIDTH, :] * xe[TAIL:TAIL + BLOCK, :] + cb
        for k in range(CONV_WIDTH - 1):
            acc = acc + cw[k:k + 1, :] * pltpu.roll(xe, CONV_WIDTH - 1 - k, 0)[TAIL:TAIL + BLOCK, :]
        xc_scr[rs, :] = acc
        for j in range(N_KV):
            v3 = jnp.concatenate([v2_scr[b, j], ones2], axis=1)
            o = _dot(p_scr[b, j], v3)
            res = o[:, 0:LANES] / o[:, LANES:2 * LANES]
            for p in range(2):
                gs = slice((2 * j + p) * LANES, (2 * j + p + 1) * LANES)
                ya_scr[rs, gs] = (res[p * BLOCK:(p + 1) * BLOCK, :] * sga_scr[rs, gs]).astype(BF16)
        gate_chunk(2 * n_chunks + b)

    z = -lam_ref[...]
    sp = jnp.maximum(z, 0.0) + jnp.log1p(jnp.exp(-jnp.abs(z)))
    for blk in range(N_RNN_BLOCKS):
        bs = slice(blk * RNN_BLOCK, (blk + 1) * RNN_BLOCK)
        xcb = xc_scr[:, bs]
        xcb16 = xcb.astype(BF16)
        r = _sigmoid(_dot(xcb16, wa_ref[blk]) + ba_ref[:, bs])
        gi = _sigmoid(_dot(xcb16, wx_ref[blk]) + bx_ref[:, bs])
        log_a = (-LRU_C * r) * sp[:, bs]
        a = jnp.exp(log_a)
        mult = jnp.sqrt(1.0 - a * a)
        bx = gi * xcb
        for b in range(BT):
            rs = slice(b * BLOCK, (b + 1) * BLOCK)
            reset = tab_scr[b, :, 3 * LANES:4 * LANES] > 0.5
            for cc in range(RNN_BLOCK // LANES):
                ls = slice(cc * LANES, (cc + 1) * LANES)
                slab = blk * (RNN_BLOCK // LANES) + cc
                a_scr[slab, b * PITCH:b * PITCH + BLOCK, :] = jnp.where(reset, 0.0, a[rs, ls])
                b_scr[slab, b * PITCH:b * PITCH + BLOCK, :] = jnp.where(reset, 1.0, mult[rs, ls]) * bx[rs, ls]
        gate_chunk(3 * n_chunks + blk)

    ya = ya_scr[...]
    steps_per_chunk = BLOCK // n_chunks
    hs = [hst_scr[c] for c in range(NG)]
    for t in range(BLOCK):
        rows_t = pl.ds(t, BT, stride=PITCH)
        for c in range(NG):
            hs[c] = a_scr[c, rows_t, :] * hs[c] + b_scr[c, rows_t, :]
            b_scr[c, rows_t, :] = hs[c]
        if (t + 1) % steps_per_chunk == 0:
            cs = slice((t // steps_per_chunk) * NC, (t // steps_per_chunk + 1) * NC)
            m1_scr[:, cs] = sma_scr[:, cs] * _dot(ya, wap_ref[:, cs])
    for c in range(NG):
        hst_scr[c] = hs[c]

    fg = fg_ref[...]
    half = BT // 2
    for hf in range(2):
        hr = slice(hf * half * BLOCK, (hf + 1) * half * BLOCK)
        for b in range(hf * half, (hf + 1) * half):
            rs = slice(b * BLOCK, (b + 1) * BLOCK)
            for g in range(NG):
                gs = slice(g * LANES, (g + 1) * LANES)
                yr_scr[rs, gs] = (b_scr[g, b * PITCH:b * PITCH + BLOCK, :] * sgr_scr[rs, gs]).astype(BF16)
        yr = yr_scr[hr, :]
        for c in range(n_chunks):
            cs = slice(c * NC, (c + 1) * NC)
            mg_scr[hr, cs] = (m1_scr[hr, cs] + smr_scr[hr, cs] * _dot(yr, wrp_ref[:, cs])).astype(BF16)
        o = _dot(mg_scr[hr, :], wout_ref[...])
        for b in range(hf * half, (hf + 1) * half):
            gate = mod_ref[0, b:b + 1, 2 * D_MODEL:3 * D_MODEL]
            y = x_ref[b] + gate * o[(b - hf * half) * BLOCK:(b - hf * half + 1) * BLOCK, :]
            r = lax.rsqrt(jnp.mean(y * y, axis=-1, keepdims=True) + NORM_EPS)
            out_ref[b] = (y * r) * fg

    k2_scr[:, :, 0:BLOCK, :] = k2_scr[:, :, BLOCK:2 * BLOCK, :]
    k2_scr[:, :, 2 * BLOCK:3 * BLOCK, :] = k2_scr[:, :, 3 * BLOCK:4 * BLOCK, :]
    not_row0 = lax.broadcasted_iota(jnp.int32, (BT, N_KV, BLOCK, LANES), 2) > 0
    zero = jnp.zeros((), BF16)
    v2_scr[:, :, 0:BLOCK, :] = jnp.where(not_row0, v2_scr[:, :, BLOCK:2 * BLOCK, :], zero)
    v2_scr[:, :, 2 * BLOCK:3 * BLOCK, :] = jnp.where(not_row0, v2_scr[:, :, 3 * BLOCK:4 * BLOCK, :], zero)
    xe_scr[:, 0:TAIL, :] = xe_scr[:, BLOCK:BLOCK + TAIL, :]


def _const_spec(shape):
    zeros = (0,) * len(shape)
    return pl.BlockSpec(shape, lambda g, s: zeros, pipeline_mode=pl.Buffered(1))


def kernel(x, c, positions, w_ada, b_ada, norm_g, w_in, attn_sinks, conv_w, conv_b, rg_wa, rg_ba, rg_wx, rg_bx, rg_lambda, w_attn_proj, w_rnn_proj, w_out, final_g):
    B, S, D = x.shape
    assert (D, S % BLOCK, B % BT) == (D_MODEL, 0, 0)
    assert w_in.shape == (1, D_MODEL, IN_WIDTH)
    n_groups, n_steps, rows = B // BT, S // BLOCK, BT * BLOCK

    mod = pl.pallas_call(
        _mod_kernel,
        grid=(3,),
        in_specs=[pl.BlockSpec((B, D), lambda i: (0, 0)),
                  pl.BlockSpec((D, D), lambda i: (0, i)),
                  pl.BlockSpec((1, D), lambda i: (0, i))],
        out_specs=pl.BlockSpec((B, D), lambda i: (0, i)),
        out_shape=jax.ShapeDtypeStruct((B, 3 * D), F32),
        name="adaln_mod",
    )(c, w_ada[0], b_ada)

    e3 = jnp.asarray(_expansion_matrix(), BF16)
    row = lambda v: v.reshape(1, -1)
    operands = (
        attn_sinks[0],
        x,
        positions.reshape(n_groups, BT, S),
        mod.reshape(n_groups, BT, 3 * D),
        norm_g, row(final_g), e3,
        w_in[0].astype(BF16), conv_w[0], conv_b, rg_wa[0].astype(BF16), rg_ba,
        rg_wx[0].astype(BF16), rg_bx, rg_lambda,
        w_attn_proj[0].astype(BF16), w_rnn_proj[0].astype(BF16), w_out[0].astype(BF16),
    )
    in_specs = [
        pl.BlockSpec(memory_space=pltpu.SMEM),
        pl.BlockSpec((BT, BLOCK, D), lambda g, s: (g, s, 0)),
        pl.BlockSpec((1, BT, BLOCK), lambda g, s: (g, 0, s)),
        pl.BlockSpec((1, BT, 3 * D), lambda g, s: (g, 0, 0)),
    ] + [_const_spec(op.shape) for op in operands[4:]]

    scratch = [
        pltpu.VMEM((rows, D), BF16),
        pltpu.VMEM((BT, N_KV, 2 * BLOCK, LANES), BF16),
        pltpu.VMEM((BT, N_KV, 4 * BLOCK, LANES), BF16),
        pltpu.VMEM((BT, N_KV, 4 * BLOCK, LANES), BF16),
        pltpu.VMEM((BT, BLOCK, 4 * LANES), F32),
        pltpu.VMEM((2 * BLOCK, 4 * BLOCK), F32),
        pltpu.VMEM((rows, D), BF16),
        pltpu.VMEM((rows, D), BF16),
        pltpu.VMEM((rows, D), BF16),
        pltpu.VMEM((rows, D), BF16),
        pltpu.VMEM((BT, N_KV, 2 * BLOCK, 4 * BLOCK), BF16),
        pltpu.VMEM((rows, D), BF16),
        pltpu.VMEM((BT, PITCH, D), F32),
        pltpu.VMEM((rows, D), F32),
        pltpu.VMEM((NG, BT * PITCH, LANES), F32),
        pltpu.VMEM((NG, BT * PITCH, LANES), F32),
        pltpu.VMEM((NG, BT, LANES), F32),
        pltpu.VMEM((rows, D), BF16),
        pltpu.VMEM((rows, D), F32),
        pltpu.VMEM((rows, D), BF16),
    ]

    out = pl.pallas_call(
        _block_kernel,
        grid=(n_groups, n_steps),
        in_specs=in_specs,
        out_specs=pl.BlockSpec((BT, BLOCK, D), lambda g, s: (g, s, 0)),
        out_shape=jax.ShapeDtypeStruct((B, S, D), F32),
        scratch_shapes=scratch,
        compiler_params=pltpu.CompilerParams(
            dimension_semantics=("arbitrary", "arbitrary"),
            vmem_limit_bytes=VMEM_LIMIT),
        name="hybrid_block",
    )(*operands)
    return out
```

```python
import math

import numpy as np
import jax
import jax.numpy as jnp
from jax import lax
from jax.experimental import pallas as pl
from jax.experimental.pallas import tpu as pltpu

F32 = jnp.float32
BF16 = jnp.bfloat16

D_MODEL = 1024
HEAD_DIM = 64
N_HEADS = 16
N_KV = 4
KV_WIDTH = N_KV * HEAD_DIM
BLOCK = 128
ROT_DIM = 16
ROPE_THETA = 500000.0
RNN_BLOCK = 256
N_RNN_BLOCKS = 4
LRU_C = 8.0
CONV_WIDTH = 4
NORM_EPS = 1e-6
NEG = -1e30
LOG2E = math.log2(math.e)
Q_SCALE = LOG2E / math.sqrt(HEAD_DIM)

LANES = 128
SUBLANES = 8
NG = D_MODEL // LANES

OFF_Q = 0
OFF_K = OFF_Q + D_MODEL
OFF_V = OFF_K + KV_WIDTH
OFF_GA = OFF_V + KV_WIDTH
OFF_XR = OFF_GA + D_MODEL
OFF_GR = OFF_XR + D_MODEL
OFF_MA = OFF_GR + D_MODEL
OFF_MR = OFF_MA + D_MODEL
IN_WIDTH = OFF_MR + D_MODEL

BT = 4
NC = 256
TAIL = SUBLANES
PITCH = BLOCK + TAIL
VMEM_LIMIT = 62 * 1024 * 1024

T_COS, T_SIN, T_RESET, T_ONE = 0, 8, 16, 17


def _expansion_matrix():
    e = np.zeros((LANES, 4 * LANES), np.float32)
    for l in range(LANES):
        d = l % HEAD_DIM
        if d < ROT_DIM:
            e[T_COS + d % 8, l] = 1.0
        else:
            e[T_ONE, l] = 1.0
        if d < 8:
            e[T_SIN + d, LANES + l] = -1.0
        elif d < ROT_DIM:
            e[T_SIN + d - 8, 2 * LANES + l] = 1.0
        e[T_RESET, 3 * LANES + l] = 1.0
    return np.concatenate([e, e], axis=0)


def _sigmoid(x):
    return 0.5 * jnp.tanh(0.5 * x) + 0.5


def _silu(x):
    u = 0.5 * x
    return u * jnp.tanh(u) + u


def _dot(a, b):
    return jnp.dot(a, b, preferred_element_type=F32)


def _rope(t, tab):
    c = tab[:, 0:LANES]
    s1 = tab[:, LANES:2 * LANES]
    s2 = tab[:, 2 * LANES:3 * LANES]
    return t * c + pltpu.roll(t, LANES - 8, 1) * s1 + pltpu.roll(t, 8, 1) * s2


def _mod_kernel(c_ref, w_ref, b_ref, o_ref):
    o_ref[...] = _dot(c_ref[...], w_ref[...]) + b_ref[...]


def _block_kernel(sinks_ref, x_ref, pos_ref, mod_ref, ng_ref, fg_ref, e2_ref,
                  win_ref, cw_ref, cb_ref, wa_ref, ba_ref, wx_ref, bx_ref, lam_ref,
                  wap_ref, wrp_ref, wout_ref,
                  out_ref,
                  h_scr, q_scr, k2_scr, v2_scr, tab_scr, mask_scr, sga_scr, sma_scr, smr_scr, sgr_scr,
                  p_scr, ya_scr, xe_scr, xc_scr, a_scr, b_scr, hst_scr, yr_scr, m1_scr, mg_scr):
    s_idx = pl.program_id(1)
    n_chunks = D_MODEL // NC

    @pl.when(s_idx == 0)
    def _init_state():
        zk = jnp.zeros((BT, N_KV, BLOCK, LANES), BF16)
        k2_scr[:, :, 0:BLOCK, :] = zk
        k2_scr[:, :, 2 * BLOCK:3 * BLOCK, :] = zk
        v2_scr[:, :, 0:BLOCK, :] = zk
        v2_scr[:, :, 2 * BLOCK:3 * BLOCK, :] = zk
        xe_scr[:, 0:TAIL, :] = jnp.zeros((BT, TAIL, D_MODEL), F32)
        hst_scr[...] = jnp.zeros((NG, BT, LANES), F32)

    qt = lax.broadcasted_iota(jnp.int32, (2 * BLOCK, 4 * BLOCK), 0) & (BLOCK - 1)
    kc = lax.broadcasted_iota(jnp.int32, (2 * BLOCK, 4 * BLOCK), 1) & (2 * BLOCK - 1)
    dist = kc - qt
    first_key = jnp.where(s_idx == 0, BLOCK, 0)
    valid = (dist >= 1) & (dist <= BLOCK) & (kc >= first_key)
    mask_scr[...] = jnp.where(valid, 0.0, NEG)

    r16 = lax.broadcasted_iota(jnp.int32, (2 * SUBLANES, LANES), 0)
    inv_freq = jnp.exp((r16 & 7).astype(F32) * (-math.log(ROPE_THETA) / 8.0))
    r8 = lax.broadcasted_iota(jnp.int32, (SUBLANES, LANES), 0)
    for b in range(BT):
        posr = pos_ref[0, b:b + 1, :]
        ang = inv_freq * posr.astype(F32)
        tab16 = jnp.where(r16 < 8, jnp.cos(ang), jnp.sin(ang))
        reset = jnp.where(posr == 0, 1.0, 0.0)
        extra = jnp.where(r8 == 0, reset, jnp.where(r8 == 1, 1.0, 0.0))
        tpad = jnp.concatenate(
            [tab16, extra, jnp.zeros((LANES - 3 * SUBLANES, LANES), F32)], axis=0)
        tt = tpad.T
        hi = tt.astype(BF16)
        lo = (tt - hi.astype(F32)).astype(BF16)
        tab_scr[b] = _dot(jnp.concatenate([hi, lo], axis=1), e2_ref[...])

    ng = ng_ref[...]
    for b in range(BT):
        xb = x_ref[b]
        r = lax.rsqrt(jnp.mean(xb * xb, axis=-1, keepdims=True) + NORM_EPS)
        shift = mod_ref[0, b:b + 1, 0:D_MODEL]
        scale = mod_ref[0, b:b + 1, D_MODEL:2 * D_MODEL]
        hb = (xb * r) * (ng * (1.0 + scale)) + shift
        h_scr[b * BLOCK:(b + 1) * BLOCK, :] = hb.astype(BF16)

    h = h_scr[...]

    for c in range(n_chunks):
        cs = slice(c * NC, (c + 1) * NC)
        xr = _dot(h, win_ref[:, OFF_XR + c * NC:OFF_XR + (c + 1) * NC])
        for b in range(BT):
            xe_scr[b, TAIL:TAIL + BLOCK, cs] = xr[b * BLOCK:(b + 1) * BLOCK, :]

    cw = cw_ref[...]
    cb = cb_ref[...]

    def conv(b):
        xe = xe_scr[b]
        acc = cw[CONV_WIDTH - 1:CONV_WIDTH, :] * xe[TAIL:TAIL + BLOCK, :] + cb
        for k in range(CONV_WIDTH - 1):
            acc = acc + cw[k:k + 1, :] * pltpu.roll(xe, CONV_WIDTH - 1 - k, 0)[TAIL:TAIL + BLOCK, :]
        xc_scr[b * BLOCK:(b + 1) * BLOCK, :] = acc

    for c in range(n_chunks):
        qc = _dot(h, win_ref[:, OFF_Q + c * NC:OFF_Q + (c + 1) * NC])
        for b in range(BT):
            tab = tab_scr[b]
            for p in range(NC // LANES):
                t = qc[b * BLOCK:(b + 1) * BLOCK, p * LANES:(p + 1) * LANES]
                q_scr[b, c, p * BLOCK:(p + 1) * BLOCK, :] = (_rope(t, tab) * Q_SCALE).astype(BF16)
        if c % 2 == 1:
            conv(c // 2)

    kv = _dot(h, win_ref[:, OFF_K:OFF_K + 2 * KV_WIDTH])
    low = lax.broadcasted_iota(jnp.int32, (BLOCK, LANES), 1) < HEAD_DIM
    for b in range(BT):
        tab = tab_scr[b]
        for u in range(KV_WIDTH // LANES):
            rs = slice(b * BLOCK, (b + 1) * BLOCK)
            kg = _rope(kv[rs, u * LANES:(u + 1) * LANES], tab)
            vg = kv[rs, KV_WIDTH + u * LANES:KV_WIDTH + (u + 1) * LANES]
            for src, src_rolled, dst in ((kg, pltpu.roll(kg, HEAD_DIM, 1), k2_scr),
                                         (vg, pltpu.roll(vg, HEAD_DIM, 1), v2_scr)):
                for hh in range(2):
                    j = 2 * u + hh
                    in_low = src if hh == 0 else src_rolled
                    in_high = src_rolled if hh == 0 else src
                    dst[b, j, BLOCK:2 * BLOCK, :] = jnp.where(low, in_low, 0.0).astype(BF16)
                    dst[b, j, 3 * BLOCK:4 * BLOCK, :] = jnp.where(low, 0.0, in_high).astype(BF16)

    row_lo = lax.broadcasted_iota(jnp.int32, (2 * BLOCK, 1), 0) < BLOCK
    col0 = lax.broadcasted_iota(jnp.int32, (2 * BLOCK, LANES), 1) == 0

    def attn_scores(b, j):
        s = lax.dot_general(q_scr[b, j], k2_scr[b, j], (((1,), (1,)), ((), ())),
                            preferred_element_type=F32)
        s = jnp.where(mask_scr[...] == 0.0, s, NEG)
        sink_a = jnp.where(row_lo, sinks_ref[4 * j] * LOG2E, sinks_ref[4 * j + 2] * LOG2E)
        sink_b = jnp.where(row_lo, sinks_ref[4 * j + 1] * LOG2E, sinks_ref[4 * j + 3] * LOG2E)
        s_a = jnp.concatenate([jnp.where(col0, sink_a, s[:, 0:LANES]), s[:, LANES:2 * LANES]], axis=1)
        s_b = jnp.concatenate([jnp.where(col0, sink_b, s[:, 2 * LANES:3 * LANES]), s[:, 3 * LANES:]], axis=1)
        m_a = jnp.max(s_a, axis=1, keepdims=True)
        m_b = jnp.max(s_b, axis=1, keepdims=True)
        p_scr[b, j] = jnp.concatenate([jnp.exp2(s_a - m_a), jnp.exp2(s_b - m_b)], axis=1).astype(BF16)

    gate_specs = ((OFF_GA, _silu, sga_scr), (OFF_MA, _sigmoid, sma_scr),
                  (OFF_MR, _sigmoid, smr_scr), (OFF_GR, _silu, sgr_scr))

    def gate_chunk(k):
        (off, act, dst), c = gate_specs[k // n_chunks], k % n_chunks
        dst[:, c * NC:(c + 1) * NC] = act(_dot(h, win_ref[:, off + c * NC:off + (c + 1) * NC])).astype(BF16)

    n_early_convs = n_chunks // 2
    for b in range(BT):
        for j in range(N_KV):
            attn_scores(b, j)
        if b + n_early_convs < BT:
            conv(b + n_early_convs)
        for kind in range(3):
            gate_chunk(kind * n_chunks + b)

    ones_r = lax.broadcasted_iota(jnp.int32, (4 * BLOCK, LANES), 0) < 2 * BLOCK
    ones_l = lax.broadcasted_iota(jnp.int32, (4 * BLOCK, LANES), 1) < HEAD_DIM
    ones2 = jnp.where(ones_r == ones_l, 1.0, 0.0).astype(BF16)
    for b in range(BT):
        rs = slice(b * BLOCK, (b + 1) * BLOCK)
        for j in range(N_KV):
            v3 = jnp.concatenate([v2_scr[b, j], ones2], axis=1)
            o = _dot(p_scr[b, j], v3)
            res = o[:, 0:LANES] / o[:, LANES:2 * LANES]
            for p in range(2):
                gs = slice((2 * j + p) * LANES, (2 * j + p + 1) * LANES)
                ya_scr[rs, gs] = (res[p * BLOCK:(p + 1) * BLOCK, :] * sga_scr[rs, gs]).astype(BF16)

    z = -lam_ref[...]
    sp = jnp.maximum(z, 0.0) + jnp.log1p(jnp.exp(-jnp.abs(z)))
    half_rate = (-0.5 * LRU_C * LOG2E) * sp
    for blk in range(N_RNN_BLOCKS):
        bs = slice(blk * RNN_BLOCK, (blk + 1) * RNN_BLOCK)
        xcb = xc_scr[:, bs]
        xcb16 = xcb.astype(BF16)
        tr = jnp.tanh(0.5 * (_dot(xcb16, wa_ref[blk]) + ba_ref[:, bs]))
        gi = _sigmoid(_dot(xcb16, wx_ref[blk]) + bx_ref[:, bs])
        a = jnp.exp2(tr * half_rate[:, bs] + half_rate[:, bs])
        v = 1.0 - a * a
        mult = jnp.where(v > 0.0, v * lax.rsqrt(v), 0.0)
        bx = gi * xcb
        for b in range(BT):
            rs = slice(b * BLOCK, (b + 1) * BLOCK)
            reset = tab_scr[b, :, 3 * LANES:4 * LANES] > 0.5
            for cc in range(RNN_BLOCK // LANES):
                ls = slice(cc * LANES, (cc + 1) * LANES)
                slab = blk * (RNN_BLOCK // LANES) + cc
                a_scr[slab, b * PITCH:b * PITCH + BLOCK, :] = jnp.where(reset, 0.0, a[rs, ls])
                b_scr[slab, b * PITCH:b * PITCH + BLOCK, :] = jnp.where(reset, 1.0, mult[rs, ls]) * bx[rs, ls]
        gate_chunk(3 * n_chunks + blk)

    ya = ya_scr[...]
    steps_per_chunk = BLOCK // n_chunks
    hs = [hst_scr[c] for c in range(NG)]
    for t in range(BLOCK):
        rows_t = pl.ds(t, BT, stride=PITCH)
        for c in range(NG):
            hs[c] = a_scr[c, rows_t, :] * hs[c] + b_scr[c, rows_t, :]
            b_scr[c, rows_t, :] = hs[c]
        if (t + 1) % steps_per_chunk == 0:
            cs = slice((t // steps_per_chunk) * NC, (t // steps_per_chunk + 1) * NC)
            m1_scr[:, cs] = sma_scr[:, cs] * _dot(ya, wap_ref[:, cs])
    for c in range(NG):
        hst_scr[c] = hs[c]

    fg = fg_ref[...]
    half = BT // 2
    for hf in range(2):
        hr = slice(hf * half * BLOCK, (hf + 1) * half * BLOCK)
        for b in range(hf * half, (hf + 1) * half):
            rs = slice(b * BLOCK, (b + 1) * BLOCK)
            for g in range(NG):
                gs = slice(g * LANES, (g + 1) * LANES)
                yr_scr[rs, gs] = (b_scr[g, b * PITCH:b * PITCH + BLOCK, :] * sgr_scr[rs, gs]).astype(BF16)
        yr = yr_scr[hr, :]
        for c in range(n_chunks):
            cs = slice(c * NC, (c + 1) * NC)
            mg_scr[hr, cs] = (m1_scr[hr, cs] + smr_scr[hr, cs] * _dot(yr, wrp_ref[:, cs])).astype(BF16)
        o = _dot(mg_scr[hr, :], wout_ref[...])
        for b in range(hf * half, (hf + 1) * half):
            gate = mod_ref[0, b:b + 1, 2 * D_MODEL:3 * D_MODEL]
            y = x_ref[b] + gate * o[(b - hf * half) * BLOCK:(b - hf * half + 1) * BLOCK, :]
            r = lax.rsqrt(jnp.mean(y * y, axis=-1, keepdims=True) + NORM_EPS)
            out_ref[b] = (y * r) * fg

    k2_scr[:, :, 0:BLOCK, :] = k2_scr[:, :, BLOCK:2 * BLOCK, :]
    k2_scr[:, :, 2 * BLOCK:3 * BLOCK, :] = k2_scr[:, :, 3 * BLOCK:4 * BLOCK, :]
    not_row0 = lax.broadcasted_iota(jnp.int32, (BT, N_KV, BLOCK, LANES), 2) > 0
    zero = jnp.zeros((), BF16)
    v2_scr[:, :, 0:BLOCK, :] = jnp.where(not_row0, v2_scr[:, :, BLOCK:2 * BLOCK, :], zero)
    v2_scr[:, :, 2 * BLOCK:3 * BLOCK, :] = jnp.where(not_row0, v2_scr[:, :, 3 * BLOCK:4 * BLOCK, :], zero)
    xe_scr[:, 0:TAIL, :] = xe_scr[:, BLOCK:BLOCK + TAIL, :]


def _const_spec(shape):
    zeros = (0,) * len(shape)
    return pl.BlockSpec(shape, lambda g, s: zeros, pipeline_mode=pl.Buffered(1))


def kernel(x, c, positions, w_ada, b_ada, norm_g, w_in, attn_sinks, conv_w, conv_b, rg_wa, rg_ba, rg_wx, rg_bx, rg_lambda, w_attn_proj, w_rnn_proj, w_out, final_g):
    B, S, D = x.shape
    assert (D, S % BLOCK, B % BT) == (D_MODEL, 0, 0)
    assert w_in.shape == (1, D_MODEL, IN_WIDTH)
    n_groups, n_steps, rows = B // BT, S // BLOCK, BT * BLOCK

    mod = pl.pallas_call(
        _mod_kernel,
        grid=(3,),
        in_specs=[pl.BlockSpec((B, D), lambda i: (0, 0)),
                  pl.BlockSpec((D, D), lambda i: (0, i)),
                  pl.BlockSpec((1, D), lambda i: (0, i))],
        out_specs=pl.BlockSpec((B, D), lambda i: (0, i)),
        out_shape=jax.ShapeDtypeStruct((B, 3 * D), F32),
        name="adaln_mod",
    )(c, w_ada[0], b_ada)

    e2 = jnp.asarray(_expansion_matrix(), BF16)
    row = lambda v: v.reshape(1, -1)
    operands = (
        attn_sinks[0],
        x,
        positions.reshape(n_groups, BT, S),
        mod.reshape(n_groups, BT, 3 * D),
        norm_g, row(final_g), e2,
        w_in[0].astype(BF16), conv_w[0], conv_b, rg_wa[0].astype(BF16), rg_ba,
        rg_wx[0].astype(BF16), rg_bx, rg_lambda,
        w_attn_proj[0].astype(BF16), w_rnn_proj[0].astype(BF16), w_out[0].astype(BF16),
    )
    in_specs = [
        pl.BlockSpec(memory_space=pltpu.SMEM),
        pl.BlockSpec((BT, BLOCK, D), lambda g, s: (g, s, 0)),
        pl.BlockSpec((1, BT, BLOCK), lambda g, s: (g, 0, s)),
        pl.BlockSpec((1, BT, 3 * D), lambda g, s: (g, 0, 0)),
    ] + [_const_spec(op.shape) for op in operands[4:]]

    scratch = [
        pltpu.VMEM((rows, D), BF16),
        pltpu.VMEM((BT, N_KV, 2 * BLOCK, LANES), BF16),
        pltpu.VMEM((BT, N_KV, 4 * BLOCK, LANES), BF16),
        pltpu.VMEM((BT, N_KV, 4 * BLOCK, LANES), BF16),
        pltpu.VMEM((BT, BLOCK, 4 * LANES), F32),
        pltpu.VMEM((2 * BLOCK, 4 * BLOCK), F32),
        pltpu.VMEM((rows, D), BF16),
        pltpu.VMEM((rows, D), BF16),
        pltpu.VMEM((rows, D), BF16),
        pltpu.VMEM((rows, D), BF16),
        pltpu.VMEM((BT, N_KV, 2 * BLOCK, 4 * BLOCK), BF16),
        pltpu.VMEM((rows, D), BF16),
        pltpu.VMEM((BT, PITCH, D), F32),
        pltpu.VMEM((rows, D), F32),
        pltpu.VMEM((NG, BT * PITCH, LANES), F32),
        pltpu.VMEM((NG, BT * PITCH, LANES), F32),
        pltpu.VMEM((NG, BT, LANES), F32),
        pltpu.VMEM((rows, D), BF16),
        pltpu.VMEM((rows, D), F32),
        pltpu.VMEM((rows, D), BF16),
    ]

    out = pl.pallas_call(
        _block_kernel,
        grid=(n_groups, n_steps),
        in_specs=in_specs,
        out_specs=pl.BlockSpec((BT, BLOCK, D), lambda g, s: (g, s, 0)),
        out_shape=jax.ShapeDtypeStruct((B, S, D), F32),
        scratch_shapes=scratch,
        compiler_params=pltpu.CompilerParams(
            dimension_semantics=("arbitrary", "arbitrary"),
            vmem_limit_bytes=VMEM_LIMIT),
        name="hybrid_block",
    )(*operands)
    return out
```

```python
import math

import numpy as np
import jax
import jax.numpy as jnp
from jax import lax
from jax.experimental import pallas as pl
from jax.experimental.pallas import tpu as pltpu

F32 = jnp.float32
BF16 = jnp.bfloat16

D_MODEL = 1024
HEAD_DIM = 64
N_HEADS = 16
N_KV = 4
KV_WIDTH = N_KV * HEAD_DIM
BLOCK = 128
ROT_DIM = 16
ROPE_THETA = 500000.0
RNN_BLOCK = 256
N_RNN_BLOCKS = 4
LRU_C = 8.0
CONV_WIDTH = 4
NORM_EPS = 1e-6
NEG = -1e30
LOG2E = math.log2(math.e)
Q_SCALE = LOG2E / math.sqrt(HEAD_DIM)

LANES = 128
SUBLANES = 8
NG = D_MODEL // LANES

OFF_Q = 0
OFF_K = OFF_Q + D_MODEL
OFF_V = OFF_K + KV_WIDTH
OFF_GA = OFF_V + KV_WIDTH
OFF_XR = OFF_GA + D_MODEL
OFF_GR = OFF_XR + D_MODEL
OFF_MA = OFF_GR + D_MODEL
OFF_MR = OFF_MA + D_MODEL
IN_WIDTH = OFF_MR + D_MODEL

BT = 4
NC = 256
N_CHUNKS = D_MODEL // NC
CHUNK_ROWS = 64
TAIL = SUBLANES
PITCH = BLOCK + TAIL
VMEM_LIMIT = 62 * 1024 * 1024

T_COS, T_SIN, T_KEEP, T_ONE = 0, 8, 16, 17


def _expansion_matrix():
    e = np.zeros((LANES, 4 * LANES), np.float32)
    for l in range(LANES):
        d = l % HEAD_DIM
        if d < ROT_DIM:
            e[T_COS + d % 8, l] = 1.0
        else:
            e[T_ONE, l] = 1.0
        if d < 8:
            e[T_SIN + d, LANES + l] = -1.0
        elif d < ROT_DIM:
            e[T_SIN + d - 8, 2 * LANES + l] = 1.0
        e[T_KEEP, 3 * LANES + l] = 1.0
    return np.concatenate([e, e], axis=0)


def _dot(a, b):
    return jnp.dot(a, b, preferred_element_type=F32)


def _rope(t, tab):
    c = tab[:, 0:LANES]
    s1 = tab[:, LANES:2 * LANES]
    s2 = tab[:, 2 * LANES:3 * LANES]
    return t * c + pltpu.roll(t, LANES - 8, 1) * s1 + pltpu.roll(t, 8, 1) * s2


def _silu_of_half(u):
    return u * jnp.tanh(u) + u


def _logistic_of_half(u):
    return 0.5 * jnp.tanh(u) + 0.5


def _mod_kernel(c_ref, w_ref, b_ref, o_ref):
    o_ref[...] = _dot(c_ref[...], w_ref[...]) + b_ref[...]


def _block_kernel(sinks_ref, x_ref, pos_ref, mod_ref, ng_ref, fg_ref, e2_ref,
                  win_ref, cw_ref, cb_ref, wa_ref, ba_ref, wx_ref, bx_ref, lam_ref,
                  wap_ref, wrp_ref, wout_ref,
                  out_ref,
                  h_scr, hh_scr, q_scr, k2_scr, v2_scr, tab_scr, mask_scr, sga_scr, sma_scr, smr_scr, sgr_scr,
                  p_scr, ya_scr, xe_scr, xc_scr, a_scr, b_scr, hst_scr, yr_scr, m1_scr, mg_scr):
    s_idx = pl.program_id(1)

    @pl.when(s_idx == 0)
    def _init_state():
        zk = jnp.zeros((BT, N_KV, BLOCK, LANES), BF16)
        k2_scr[:, :, 0:BLOCK, :] = zk
        k2_scr[:, :, 2 * BLOCK:3 * BLOCK, :] = zk
        v2_scr[:, :, 0:BLOCK, :] = zk
        v2_scr[:, :, 2 * BLOCK:3 * BLOCK, :] = zk
        xe_scr[:, 0:TAIL, :] = jnp.zeros((BT, TAIL, D_MODEL), F32)
        hst_scr[...] = jnp.zeros((NG, BT, LANES), F32)

    qt = lax.broadcasted_iota(jnp.int32, (2 * BLOCK, 4 * BLOCK), 0) & (BLOCK - 1)
    kc = lax.broadcasted_iota(jnp.int32, (2 * BLOCK, 4 * BLOCK), 1) & (2 * BLOCK - 1)
    dist = kc - qt
    first_key = jnp.where(s_idx == 0, BLOCK, 0)
    valid = (dist >= 1) & (dist <= BLOCK) & (kc >= first_key)
    mask_scr[...] = jnp.where(valid, jnp.inf, NEG)

    r16 = lax.broadcasted_iota(jnp.int32, (2 * SUBLANES, LANES), 0)
    inv_freq = jnp.exp((r16 & 7).astype(F32) * (-math.log(ROPE_THETA) / 8.0))
    r8 = lax.broadcasted_iota(jnp.int32, (SUBLANES, LANES), 0)
    for b in range(BT):
        posr = pos_ref[0, b:b + 1, :]
        ang = inv_freq * posr.astype(F32)
        tab16 = jnp.where(r16 < 8, jnp.cos(ang), jnp.sin(ang))
        keep = jnp.where(posr == 0, 0.0, 1.0)
        extra = jnp.where(r8 == 0, keep, jnp.where(r8 == 1, 1.0, 0.0))
        tpad = jnp.concatenate(
            [tab16, extra, jnp.zeros((LANES - 3 * SUBLANES, LANES), F32)], axis=0)
        tt = tpad.T
        hi = tt.astype(BF16)
        lo = (tt - hi.astype(F32)).astype(BF16)
        tab_scr[b] = _dot(jnp.concatenate([hi, lo], axis=1), e2_ref[...])

    ng = ng_ref[...]
    for b in range(BT):
        xb = x_ref[b]
        r = lax.rsqrt(jnp.mean(xb * xb, axis=-1, keepdims=True) + NORM_EPS)
        shift = mod_ref[0, b:b + 1, 0:D_MODEL]
        scale = mod_ref[0, b:b + 1, D_MODEL:2 * D_MODEL]
        hb = (xb * r) * (ng * (1.0 + scale)) + shift
        h_scr[b * BLOCK:(b + 1) * BLOCK, :] = hb.astype(BF16)
        hh_scr[b * BLOCK:(b + 1) * BLOCK, :] = (0.5 * hb).astype(BF16)

    h = h_scr[...]
    h_half = hh_scr[...]


    def proj_xr(c):
        xr = _dot(h, win_ref[:, OFF_XR + c * NC:OFF_XR + (c + 1) * NC])
        for b in range(BT):
            xe_scr[b, TAIL:TAIL + BLOCK, c * NC:(c + 1) * NC] = xr[b * BLOCK:(b + 1) * BLOCK, :]

    cw = cw_ref[...]
    cb = cb_ref[...]

    def conv(b):
        for g in range(NG):
            gs = slice(g * LANES, (g + 1) * LANES)
            xe = xe_scr[b, :, gs]
            acc = cw[CONV_WIDTH - 1:CONV_WIDTH, gs] * xe[TAIL:TAIL + BLOCK, :] + cb[:, gs]
            for k in range(CONV_WIDTH - 1):
                acc = acc + cw[k:k + 1, gs] * pltpu.roll(xe, CONV_WIDTH - 1 - k, 0)[TAIL:TAIL + BLOCK, :]
            xc_scr[b * BLOCK:(b + 1) * BLOCK, gs] = acc

    def proj_q(c):
        qc = _dot(h, win_ref[:, OFF_Q + c * NC:OFF_Q + (c + 1) * NC])
        for b in range(BT):
            tab = tab_scr[b]
            for p in range(NC // LANES):
                t = qc[b * BLOCK:(b + 1) * BLOCK, p * LANES:(p + 1) * LANES]
                q_scr[b, c, p * BLOCK:(p + 1) * BLOCK, :] = (_rope(t, tab) * Q_SCALE).astype(BF16)

    def proj_kv():
        kv = _dot(h, win_ref[:, OFF_K:OFF_K + 2 * KV_WIDTH])
        low = lax.broadcasted_iota(jnp.int32, (BLOCK, LANES), 1) < HEAD_DIM
        for b in range(BT):
            tab = tab_scr[b]
            rs = slice(b * BLOCK, (b + 1) * BLOCK)
            for u in range(KV_WIDTH // LANES):
                kg = _rope(kv[rs, u * LANES:(u + 1) * LANES], tab)
                vg = kv[rs, KV_WIDTH + u * LANES:KV_WIDTH + (u + 1) * LANES]
                for src, src_rolled, dst in ((kg, pltpu.roll(kg, HEAD_DIM, 1), k2_scr),
                                             (vg, pltpu.roll(vg, HEAD_DIM, 1), v2_scr)):
                    for side in range(2):
                        j = 2 * u + side
                        in_low = src if side == 0 else src_rolled
                        in_high = src_rolled if side == 0 else src
                        dst[b, j, BLOCK:2 * BLOCK, :] = jnp.where(low, in_low, 0.0).astype(BF16)
                        dst[b, j, 3 * BLOCK:4 * BLOCK, :] = jnp.where(low, 0.0, in_high).astype(BF16)

    col0 = lax.broadcasted_iota(jnp.int32, (CHUNK_ROWS, LANES), 1) == 0

    def attn_scores(b, j):
        s = lax.dot_general(q_scr[b, j], k2_scr[b, j], (((1,), (1,)), ((), ())),
                            preferred_element_type=F32)
        for r0 in range(0, 2 * BLOCK, CHUNK_ROWS):
            rs = slice(r0, r0 + CHUNK_ROWS)
            pair = r0 // BLOCK
            sc = jnp.minimum(s[rs, :], mask_scr[rs, :])
            sink_a = sinks_ref[4 * j + 2 * pair] * LOG2E
            sink_b = sinks_ref[4 * j + 2 * pair + 1] * LOG2E
            s_a = jnp.concatenate([jnp.where(col0, sink_a, sc[:, 0:LANES]), sc[:, LANES:2 * LANES]], axis=1)
            s_b = jnp.concatenate([jnp.where(col0, sink_b, sc[:, 2 * LANES:3 * LANES]), sc[:, 3 * LANES:]], axis=1)
            m_a = jnp.max(s_a, axis=1, keepdims=True)
            m_b = jnp.max(s_b, axis=1, keepdims=True)
            p_scr[b, j, rs, :] = jnp.concatenate(
                [jnp.exp2(s_a - m_a), jnp.exp2(s_b - m_b)], axis=1).astype(BF16)

    gate_specs = ((OFF_GA, _silu_of_half, sga_scr), (OFF_MA, _logistic_of_half, sma_scr),
                  (OFF_MR, _logistic_of_half, smr_scr), (OFF_GR, _silu_of_half, sgr_scr))
    GATE_ATTN, GATE_MERGE_ATTN, GATE_MERGE_RNN, GATE_RNN = range(4)

    def gate_chunk(kind, c):
        off, act, dst = gate_specs[kind]
        u = _dot(h_half, win_ref[:, off + c * NC:off + (c + 1) * NC])
        for r0 in range(0, BT * BLOCK, CHUNK_ROWS):
            for cc in range(NC // LANES):
                ls = slice(cc * LANES, (cc + 1) * LANES)
                dst[r0:r0 + CHUNK_ROWS, c * NC + cc * LANES:c * NC + (cc + 1) * LANES] = (
                    act(u[r0:r0 + CHUNK_ROWS, ls]).astype(BF16))

    ones_r = lax.broadcasted_iota(jnp.int32, (4 * BLOCK, LANES), 0) < 2 * BLOCK
    ones_l = lax.broadcasted_iota(jnp.int32, (4 * BLOCK, LANES), 1) < HEAD_DIM
    ones2 = jnp.where(ones_r == ones_l, 1.0, 0.0).astype(BF16)

    def attn_values(b):
        rs = slice(b * BLOCK, (b + 1) * BLOCK)
        for j in range(N_KV):
            v3 = jnp.concatenate([v2_scr[b, j], ones2], axis=1)
            o = _dot(p_scr[b, j], v3)
            res = o[:, 0:LANES] / o[:, LANES:2 * LANES]
            for p in range(2):
                gs = slice((2 * j + p) * LANES, (2 * j + p + 1) * LANES)
                ya_scr[rs, gs] = (res[p * BLOCK:(p + 1) * BLOCK, :] * sga_scr[rs, gs]).astype(BF16)

    z = -lam_ref[...]
    sp = jnp.maximum(z, 0.0) + jnp.log1p(jnp.exp(-jnp.abs(z)))
    half_rate = (-0.5 * LRU_C * LOG2E) * sp

    def lru_gates(blk):
        bs = slice(blk * RNN_BLOCK, (blk + 1) * RNN_BLOCK)
        xh16 = (0.5 * xc_scr[:, bs]).astype(BF16)
        ga = _dot(xh16, wa_ref[blk]) + 0.5 * ba_ref[:, bs]
        gx = _dot(xh16, wx_ref[blk]) + 0.5 * bx_ref[:, bs]
        for b in range(BT):
            for cc in range(RNN_BLOCK // LANES):
                slab = blk * (RNN_BLOCK // LANES) + cc
                ls = slice(cc * LANES, (cc + 1) * LANES)
                cs = slice(slab * LANES, (slab + 1) * LANES)
                for r0 in range(0, BLOCK, CHUNK_ROWS):
                    rs = slice(b * BLOCK + r0, b * BLOCK + r0 + CHUNK_ROWS)
                    ds = slice(b * PITCH + r0, b * PITCH + r0 + CHUNK_ROWS)
                    a = (jnp.exp2(jnp.tanh(ga[rs, ls]) * half_rate[:, cs] + half_rate[:, cs])
                         * tab_scr[b, r0:r0 + CHUNK_ROWS, 3 * LANES:4 * LANES])
                    v = 1.0 - a * a
                    mult = jnp.where(v > 0.0, v * lax.rsqrt(v), 0.0)
                    gi = _logistic_of_half(gx[rs, ls])
                    a_scr[slab, ds, :] = a
                    b_scr[slab, ds, :] = mult * (gi * xc_scr[rs, cs])

    hs = [hst_scr[c] for c in range(NG)]

    def scan_steps(t0, t1):
        for t in range(t0, t1):
            rows_t = pl.ds(t, BT, stride=PITCH)
            for c in range(NG):
                hs[c] = a_scr[c, rows_t, :] * hs[c] + b_scr[c, rows_t, :]
                b_scr[c, rows_t, :] = hs[c]

    def attn_proj(c):
        cs = slice(c * NC, (c + 1) * NC)
        m1_scr[:, cs] = sma_scr[:, cs] * _dot(ya_scr[...], wap_ref[:, cs])

    fg = fg_ref[...]
    half = BT // 2

    def tail(hf):
        hr = slice(hf * half * BLOCK, (hf + 1) * half * BLOCK)
        for b in range(hf * half, (hf + 1) * half):
            rs = slice(b * BLOCK, (b + 1) * BLOCK)
            for g in range(NG):
                gs = slice(g * LANES, (g + 1) * LANES)
                yr_scr[rs, gs] = (b_scr[g, b * PITCH:b * PITCH + BLOCK, :] * sgr_scr[rs, gs]).astype(BF16)
        yr = yr_scr[hr, :]
        for c in range(N_CHUNKS):
            cs = slice(c * NC, (c + 1) * NC)
            mg_scr[hr, cs] = (m1_scr[hr, cs] + smr_scr[hr, cs] * _dot(yr, wrp_ref[:, cs])).astype(BF16)
        o = _dot(mg_scr[hr, :], wout_ref[...])
        for b in range(hf * half, (hf + 1) * half):
            gate = mod_ref[0, b:b + 1, 2 * D_MODEL:3 * D_MODEL]
            y = x_ref[b] + gate * o[(b - hf * half) * BLOCK:(b - hf * half + 1) * BLOCK, :]
            r = lax.rsqrt(jnp.mean(y * y, axis=-1, keepdims=True) + NORM_EPS)
            out_ref[b] = (y * r) * fg

    for c in range(N_CHUNKS):
        proj_xr(c)
    for c in range(N_CHUNKS):
        proj_q(c)
        if c % 2 == 1:
            conv(c // 2)
    proj_kv()
    for b in range(BT):
        for j in range(N_KV):
            attn_scores(b, j)
        if b + N_CHUNKS // 2 < BT:
            conv(b + N_CHUNKS // 2)
        for kind in (GATE_ATTN, GATE_MERGE_ATTN, GATE_MERGE_RNN):
            gate_chunk(kind, b)
    for b in range(BT):
        attn_values(b)
    for i in range(N_RNN_BLOCKS):
        lru_gates(i)
        gate_chunk(GATE_RNN, i)
    steps = BLOCK // N_CHUNKS
    for c in range(N_CHUNKS):
        scan_steps(c * steps, (c + 1) * steps)
        attn_proj(c)
    for c in range(NG):
        hst_scr[c] = hs[c]
    for hf in range(2):
        tail(hf)

    k2_scr[:, :, 0:BLOCK, :] = k2_scr[:, :, BLOCK:2 * BLOCK, :]
    k2_scr[:, :, 2 * BLOCK:3 * BLOCK, :] = k2_scr[:, :, 3 * BLOCK:4 * BLOCK, :]
    not_row0 = lax.broadcasted_iota(jnp.int32, (BT, N_KV, BLOCK, LANES), 2) > 0
    zero = jnp.zeros((), BF16)
    v2_scr[:, :, 0:BLOCK, :] = jnp.where(not_row0, v2_scr[:, :, BLOCK:2 * BLOCK, :], zero)
    v2_scr[:, :, 2 * BLOCK:3 * BLOCK, :] = jnp.where(not_row0, v2_scr[:, :, 3 * BLOCK:4 * BLOCK, :], zero)
    xe_scr[:, 0:TAIL, :] = xe_scr[:, BLOCK:BLOCK + TAIL, :]


def _const_spec(shape):
    zeros = (0,) * len(shape)
    return pl.BlockSpec(shape, lambda g, s: zeros, pipeline_mode=pl.Buffered(1))


def kernel(x, c, positions, w_ada, b_ada, norm_g, w_in, attn_sinks, conv_w, conv_b, rg_wa, rg_ba, rg_wx, rg_bx, rg_lambda, w_attn_proj, w_rnn_proj, w_out, final_g):
    B, S, D = x.shape
    assert (D, S % BLOCK, B % BT) == (D_MODEL, 0, 0)
    assert w_in.shape == (1, D_MODEL, IN_WIDTH)
    assert BT == N_CHUNKS == N_KV == N_RNN_BLOCKS
    n_groups, n_steps, rows = B // BT, S // BLOCK, BT * BLOCK

    mod = pl.pallas_call(
        _mod_kernel,
        grid=(3,),
        in_specs=[pl.BlockSpec((B, D), lambda i: (0, 0)),
                  pl.BlockSpec((D, D), lambda i: (0, i)),
                  pl.BlockSpec((1, D), lambda i: (0, i))],
        out_specs=pl.BlockSpec((B, D), lambda i: (0, i)),
        out_shape=jax.ShapeDtypeStruct((B, 3 * D), F32),
        name="adaln_mod",
    )(c, w_ada[0], b_ada)

    e2 = jnp.asarray(_expansion_matrix(), BF16)
    row = lambda v: v.reshape(1, -1)
    operands = (
        attn_sinks[0],
        x,
        positions.reshape(n_groups, BT, S),
        mod.reshape(n_groups, BT, 3 * D),
        norm_g, row(final_g), e2,
        w_in[0].astype(BF16), conv_w[0], conv_b, rg_wa[0].astype(BF16), rg_ba,
        rg_wx[0].astype(BF16), rg_bx, rg_lambda,
        w_attn_proj[0].astype(BF16), w_rnn_proj[0].astype(BF16), w_out[0].astype(BF16),
    )
    in_specs = [
        pl.BlockSpec(memory_space=pltpu.SMEM),
        pl.BlockSpec((BT, BLOCK, D), lambda g, s: (g, s, 0)),
        pl.BlockSpec((1, BT, BLOCK), lambda g, s: (g, 0, s)),
        pl.BlockSpec((1, BT, 3 * D), lambda g, s: (g, 0, 0)),
    ] + [_const_spec(op.shape) for op in operands[4:]]

    scratch = [
        pltpu.VMEM((rows, D), BF16),
        pltpu.VMEM((rows, D), BF16),
        pltpu.VMEM((BT, N_KV, 2 * BLOCK, LANES), BF16),
        pltpu.VMEM((BT, N_KV, 4 * BLOCK, LANES), BF16),
        pltpu.VMEM((BT, N_KV, 4 * BLOCK, LANES), BF16),
        pltpu.VMEM((BT, BLOCK, 4 * LANES), F32),
        pltpu.VMEM((2 * BLOCK, 4 * BLOCK), F32),
        pltpu.VMEM((rows, D), BF16),
        pltpu.VMEM((rows, D), BF16),
        pltpu.VMEM((rows, D), BF16),
        pltpu.VMEM((rows, D), BF16),
        pltpu.VMEM((BT, N_KV, 2 * BLOCK, 4 * BLOCK), BF16),
        pltpu.VMEM((rows, D), BF16),
        pltpu.VMEM((BT, PITCH, D), F32),
        pltpu.VMEM((rows, D), F32),
        pltpu.VMEM((NG, BT * PITCH, LANES), F32),
        pltpu.VMEM((NG, BT * PITCH, LANES), F32),
        pltpu.VMEM((NG, BT, LANES), F32),
        pltpu.VMEM((rows, D), BF16),
        pltpu.VMEM((rows, D), F32),
        pltpu.VMEM((rows, D), BF16),
    ]

    out = pl.pallas_call(
        _block_kernel,
        grid=(n_groups, n_steps),
        in_specs=in_specs,
        out_specs=pl.BlockSpec((BT, BLOCK, D), lambda g, s: (g, s, 0)),
        out_shape=jax.ShapeDtypeStruct((B, S, D), F32),
        scratch_shapes=scratch,
        compiler_params=pltpu.CompilerParams(
            dimension_semantics=("arbitrary", "arbitrary"),
            vmem_limit_bytes=VMEM_LIMIT),
        name="hybrid_block",
    )(*operands)
    return out
```

```python
import math

import numpy as np
import jax
import jax.numpy as jnp
from jax import lax
from jax.experimental import pallas as pl
from jax.experimental.pallas import tpu as pltpu

F32 = jnp.float32
BF16 = jnp.bfloat16

D_MODEL = 1024
HEAD_DIM = 64
N_HEADS = 16
N_KV = 4
Q_PER_KV = N_HEADS // N_KV
KV_WIDTH = N_KV * HEAD_DIM
BLOCK = 128
ROT_DIM = 16
ROT_HALF = ROT_DIM // 2
ROPE_THETA = 500000.0
RNN_BLOCK = 256
N_RNN_BLOCKS = 4
LRU_C = 8.0
CONV_WIDTH = 4
NORM_EPS = 1e-6
NEG = -1e30
LOG2E = math.log2(math.e)
Q_SCALE = 4.0 * LOG2E / math.sqrt(HEAD_DIM)

LANES = 128
SUBLANES = 8
NG = D_MODEL // LANES

OFF_Q = 0
OFF_K = OFF_Q + D_MODEL
OFF_V = OFF_K + KV_WIDTH
OFF_GA = OFF_V + KV_WIDTH
OFF_XR = OFF_GA + D_MODEL
OFF_GR = OFF_XR + D_MODEL
OFF_MA = OFF_GR + D_MODEL
OFF_MR = OFF_MA + D_MODEL
IN_WIDTH = OFF_MR + D_MODEL

BT = 4
NC = 256
N_CHUNKS = D_MODEL // NC
CHUNK_ROWS = 64
TAIL = SUBLANES
PITCH = BLOCK + TAIL
VMEM_LIMIT = 62 * 1024 * 1024

T_COS, T_SIN, T_KEEP, T_ONE = 0, ROT_HALF, ROT_DIM, ROT_DIM + 1


def _expansion_matrix():
    e = np.zeros((LANES, 4 * LANES), np.float32)
    for l in range(LANES):
        d = l % HEAD_DIM
        if d < ROT_DIM:
            e[T_COS + d % ROT_HALF, l] = 1.0
        else:
            e[T_ONE, l] = 1.0
        if d < ROT_HALF:
            e[T_SIN + d, LANES + l] = -1.0
        elif d < ROT_DIM:
            e[T_SIN + d - ROT_HALF, 2 * LANES + l] = 1.0
        e[T_KEEP, 3 * LANES + l] = 1.0
    return np.concatenate([e, e], axis=0)


def _dot(a, b):
    return jnp.dot(a, b, preferred_element_type=F32)


def _rope(t, tab):
    c = tab[:, 0:LANES]
    s1 = tab[:, LANES:2 * LANES]
    s2 = tab[:, 2 * LANES:3 * LANES]
    return t * c + pltpu.roll(t, LANES - ROT_HALF, 1) * s1 + pltpu.roll(t, ROT_HALF, 1) * s2


def _silu_of_half(u):
    return u * jnp.tanh(u) + u


def _logistic_of_half(u):
    return 0.5 * jnp.tanh(u) + 0.5


def _mod_kernel(c_ref, w_ref, b_ref, o_ref):
    o_ref[...] = _dot(c_ref[...], w_ref[...]) + b_ref[...]


def _block_kernel(sinks_ref, x_ref, pos_ref, mod_ref, ng_ref, fg_ref, e2_ref,
                  win_ref, cw_ref, cb_ref, wa_ref, ba_ref, wx_ref, bx_ref, lam_ref,
                  wap_ref, wrp_ref, wout_ref,
                  out_ref,
                  hh_scr, q_scr, k2_scr, v2_scr, tab_scr, mask_scr, sga_scr, sma_scr, smr_scr, sgr_scr,
                  p_scr, ya_scr, xe_scr, xc_scr, a_scr, b_scr, hst_scr, yr_scr, m1_scr, mg_scr):
    s_idx = pl.program_id(1)

    @pl.when(s_idx == 0)
    def _init_state():
        zk = jnp.zeros((BT, N_KV, BLOCK, LANES), BF16)
        k2_scr[:, :, 0:BLOCK, :] = zk
        k2_scr[:, :, 2 * BLOCK:3 * BLOCK, :] = zk
        v2_scr[:, :, 0:BLOCK, :] = zk
        v2_scr[:, :, 2 * BLOCK:3 * BLOCK, :] = zk
        xe_scr[:, 0:TAIL, :] = jnp.zeros((BT, TAIL, D_MODEL), F32)
        hst_scr[...] = jnp.zeros((NG, BT, LANES), F32)

    qt = lax.broadcasted_iota(jnp.int32, (2 * BLOCK, 4 * BLOCK), 0) & (BLOCK - 1)
    kc = lax.broadcasted_iota(jnp.int32, (2 * BLOCK, 4 * BLOCK), 1) & (2 * BLOCK - 1)
    dist = kc - qt
    first_key = jnp.where(s_idx == 0, BLOCK, 0)
    valid = (dist >= 1) & (dist <= BLOCK) & (kc >= first_key)
    mask_scr[...] = jnp.where(valid, jnp.inf, NEG)

    r16 = lax.broadcasted_iota(jnp.int32, (2 * SUBLANES, LANES), 0)
    inv_freq = jnp.exp((r16 & (ROT_HALF - 1)).astype(F32) * (-math.log(ROPE_THETA) / ROT_HALF))
    r8 = lax.broadcasted_iota(jnp.int32, (SUBLANES, LANES), 0)
    for b in range(BT):
        posr = pos_ref[0, b:b + 1, :]
        ang = inv_freq * posr.astype(F32)
        tab16 = jnp.where(r16 < ROT_HALF, jnp.cos(ang), jnp.sin(ang))
        keep = jnp.where(posr == 0, 0.0, 1.0)
        extra = jnp.where(r8 == 0, keep, jnp.where(r8 == 1, 1.0, 0.0))
        tpad = jnp.concatenate(
            [tab16, extra, jnp.zeros((LANES - 3 * SUBLANES, LANES), F32)], axis=0)
        tt = tpad.T
        hi = tt.astype(BF16)
        lo = (tt - hi.astype(F32)).astype(BF16)
        tab_scr[b] = _dot(jnp.concatenate([hi, lo], axis=1), e2_ref[...])

    ng = ng_ref[...]
    for b in range(BT):
        xb = x_ref[b]
        r = lax.rsqrt(jnp.mean(xb * xb, axis=-1, keepdims=True) + NORM_EPS)
        shift = mod_ref[0, b:b + 1, 0:D_MODEL]
        scale = mod_ref[0, b:b + 1, D_MODEL:2 * D_MODEL]
        hb = (xb * r) * (ng * (1.0 + scale)) + shift
        hh_scr[b * BLOCK:(b + 1) * BLOCK, :] = (0.5 * hb).astype(BF16)

    h_half = hh_scr[...]


    def proj_xr(c):
        xr = _dot(h_half, win_ref[:, OFF_XR + c * NC:OFF_XR + (c + 1) * NC])
        for b in range(BT):
            xe_scr[b, TAIL:TAIL + BLOCK, c * NC:(c + 1) * NC] = xr[b * BLOCK:(b + 1) * BLOCK, :]

    cw = cw_ref[...]
    half_cb = 0.5 * cb_ref[...]

    def conv(b):
        for g in range(NG):
            gs = slice(g * LANES, (g + 1) * LANES)
            xe = xe_scr[b, :, gs]
            acc = cw[CONV_WIDTH - 1:CONV_WIDTH, gs] * xe[TAIL:TAIL + BLOCK, :] + half_cb[:, gs]
            for k in range(CONV_WIDTH - 1):
                acc = acc + cw[k:k + 1, gs] * pltpu.roll(xe, CONV_WIDTH - 1 - k, 0)[TAIL:TAIL + BLOCK, :]
            xc_scr[b * BLOCK:(b + 1) * BLOCK, gs] = acc

    def proj_q(c):
        qc = _dot(h_half, win_ref[:, OFF_Q + c * NC:OFF_Q + (c + 1) * NC])
        for b in range(BT):
            tab = tab_scr[b]
            for p in range(NC // LANES):
                t = qc[b * BLOCK:(b + 1) * BLOCK, p * LANES:(p + 1) * LANES]
                q_scr[b, c, p * BLOCK:(p + 1) * BLOCK, :] = (_rope(t, tab) * Q_SCALE).astype(BF16)

    def proj_kv():
        kv = _dot(h_half, win_ref[:, OFF_K:OFF_K + 2 * KV_WIDTH])
        low = lax.broadcasted_iota(jnp.int32, (BLOCK, LANES), 1) < HEAD_DIM
        for b in range(BT):
            tab = tab_scr[b]
            rs = slice(b * BLOCK, (b + 1) * BLOCK)
            for u in range(KV_WIDTH // LANES):
                kg = _rope(kv[rs, u * LANES:(u + 1) * LANES], tab)
                vg = kv[rs, KV_WIDTH + u * LANES:KV_WIDTH + (u + 1) * LANES]
                for src, src_rolled, dst in ((kg, pltpu.roll(kg, HEAD_DIM, 1), k2_scr),
                                             (vg, pltpu.roll(vg, HEAD_DIM, 1), v2_scr)):
                    for side in range(2):
                        j = 2 * u + side
                        in_low = src if side == 0 else src_rolled
                        in_high = src_rolled if side == 0 else src
                        dst[b, j, BLOCK:2 * BLOCK, :] = jnp.where(low, in_low, 0.0).astype(BF16)
                        dst[b, j, 3 * BLOCK:4 * BLOCK, :] = jnp.where(low, 0.0, in_high).astype(BF16)

    col0 = lax.broadcasted_iota(jnp.int32, (CHUNK_ROWS, LANES), 1) == 0

    def attn_scores(b, j):
        s = lax.dot_general(q_scr[b, j], k2_scr[b, j], (((1,), (1,)), ((), ())),
                            preferred_element_type=F32)
        for r0 in range(0, 2 * BLOCK, CHUNK_ROWS):
            rs = slice(r0, r0 + CHUNK_ROWS)
            pair = r0 // BLOCK
            sc = jnp.minimum(s[rs, :], mask_scr[rs, :])
            sink_a = sinks_ref[Q_PER_KV * j + 2 * pair] * LOG2E
            sink_b = sinks_ref[Q_PER_KV * j + 2 * pair + 1] * LOG2E
            s_a = jnp.concatenate([jnp.where(col0, sink_a, sc[:, 0:LANES]), sc[:, LANES:2 * LANES]], axis=1)
            s_b = jnp.concatenate([jnp.where(col0, sink_b, sc[:, 2 * LANES:3 * LANES]), sc[:, 3 * LANES:]], axis=1)
            m_a = jnp.max(s_a, axis=1, keepdims=True)
            m_b = jnp.max(s_b, axis=1, keepdims=True)
            p_scr[b, j, rs, :] = jnp.concatenate(
                [jnp.exp2(s_a - m_a), jnp.exp2(s_b - m_b)], axis=1).astype(BF16)

    gate_specs = ((OFF_GA, _silu_of_half, sga_scr), (OFF_MA, _logistic_of_half, sma_scr),
                  (OFF_MR, _logistic_of_half, smr_scr), (OFF_GR, _silu_of_half, sgr_scr))
    GATE_ATTN, GATE_MERGE_ATTN, GATE_MERGE_RNN, GATE_RNN = range(4)

    def gate_chunk(kind, c):
        off, act, dst = gate_specs[kind]
        u = _dot(h_half, win_ref[:, off + c * NC:off + (c + 1) * NC])
        for r0 in range(0, BT * BLOCK, CHUNK_ROWS):
            for cc in range(NC // LANES):
                ls = slice(cc * LANES, (cc + 1) * LANES)
                dst[r0:r0 + CHUNK_ROWS, c * NC + cc * LANES:c * NC + (cc + 1) * LANES] = (
                    act(u[r0:r0 + CHUNK_ROWS, ls]).astype(BF16))

    ones_r = lax.broadcasted_iota(jnp.int32, (4 * BLOCK, LANES), 0) < 2 * BLOCK
    ones_l = lax.broadcasted_iota(jnp.int32, (4 * BLOCK, LANES), 1) < HEAD_DIM
    ones2 = jnp.where(ones_r == ones_l, 0.5, 0.0).astype(BF16)

    def attn_values(b):
        rs = slice(b * BLOCK, (b + 1) * BLOCK)
        for j in range(N_KV):
            v3 = jnp.concatenate([v2_scr[b, j], ones2], axis=1)
            o = _dot(p_scr[b, j], v3)
            res = o[:, 0:LANES] / o[:, LANES:2 * LANES]
            for p in range(2):
                gs = slice((2 * j + p) * LANES, (2 * j + p + 1) * LANES)
                ya_scr[rs, gs] = (res[p * BLOCK:(p + 1) * BLOCK, :] * sga_scr[rs, gs]).astype(BF16)

    z = -lam_ref[...]
    sp = jnp.maximum(z, 0.0) + jnp.log1p(jnp.exp(-jnp.abs(z)))
    half_rate = (-0.5 * LRU_C * LOG2E) * sp

    def lru_gates(blk):
        bs = slice(blk * RNN_BLOCK, (blk + 1) * RNN_BLOCK)
        xh16 = xc_scr[:, bs].astype(BF16)
        ga = _dot(xh16, wa_ref[blk]) + 0.5 * ba_ref[:, bs]
        gx = _dot(xh16, wx_ref[blk]) + 0.5 * bx_ref[:, bs]
        for b in range(BT):
            for cc in range(RNN_BLOCK // LANES):
                slab = blk * (RNN_BLOCK // LANES) + cc
                ls = slice(cc * LANES, (cc + 1) * LANES)
                cs = slice(slab * LANES, (slab + 1) * LANES)
                for r0 in range(0, BLOCK, CHUNK_ROWS):
                    rs = slice(b * BLOCK + r0, b * BLOCK + r0 + CHUNK_ROWS)
                    ds = slice(b * PITCH + r0, b * PITCH + r0 + CHUNK_ROWS)
                    a = (jnp.exp2(jnp.tanh(ga[rs, ls]) * half_rate[:, cs] + half_rate[:, cs])
                         * tab_scr[b, r0:r0 + CHUNK_ROWS, 3 * LANES:4 * LANES])
                    v = 1.0 - a * a
                    mult = jnp.where(v > 0.0, v * lax.rsqrt(v), 0.0)
                    gi2 = jnp.tanh(gx[rs, ls]) + 1.0
                    a_scr[slab, ds, :] = a
                    b_scr[slab, ds, :] = mult * (gi2 * xc_scr[rs, cs])

    hs = [hst_scr[c] for c in range(NG)]

    def scan_steps(t0, t1):
        for t in range(t0, t1):
            rows_t = pl.ds(t, BT, stride=PITCH)
            for c in range(NG):
                hs[c] = a_scr[c, rows_t, :] * hs[c] + b_scr[c, rows_t, :]
                b_scr[c, rows_t, :] = hs[c]

    def attn_proj(c):
        cs = slice(c * NC, (c + 1) * NC)
        m1_scr[:, cs] = sma_scr[:, cs] * _dot(ya_scr[...], wap_ref[:, cs])

    fg = fg_ref[...]
    half = BT // 2

    def tail(hf):
        hr = slice(hf * half * BLOCK, (hf + 1) * half * BLOCK)
        for b in range(hf * half, (hf + 1) * half):
            rs = slice(b * BLOCK, (b + 1) * BLOCK)
            for g in range(NG):
                gs = slice(g * LANES, (g + 1) * LANES)
                yr_scr[rs, gs] = (b_scr[g, b * PITCH:b * PITCH + BLOCK, :] * sgr_scr[rs, gs]).astype(BF16)
        yr = yr_scr[hr, :]
        for c in range(N_CHUNKS):
            cs = slice(c * NC, (c + 1) * NC)
            mg_scr[hr, cs] = (m1_scr[hr, cs] + smr_scr[hr, cs] * _dot(yr, wrp_ref[:, cs])).astype(BF16)
        o = _dot(mg_scr[hr, :], wout_ref[...])
        for b in range(hf * half, (hf + 1) * half):
            gate = mod_ref[0, b:b + 1, 2 * D_MODEL:3 * D_MODEL]
            y = x_ref[b] + gate * o[(b - hf * half) * BLOCK:(b - hf * half + 1) * BLOCK, :]
            r = lax.rsqrt(jnp.mean(y * y, axis=-1, keepdims=True) + NORM_EPS)
            out_ref[b] = (y * r) * fg

    for c in range(N_CHUNKS):
        proj_xr(c)
    for c in range(N_CHUNKS):
        proj_q(c)
        if c % 2 == 1:
            conv(c // 2)
    proj_kv()
    for b in range(BT):
        for j in range(N_KV):
            attn_scores(b, j)
        if b + N_CHUNKS // 2 < BT:
            conv(b + N_CHUNKS // 2)
        for kind in (GATE_ATTN, GATE_MERGE_ATTN, GATE_MERGE_RNN):
            gate_chunk(kind, b)
    for b in range(BT):
        attn_values(b)
    for i in range(N_RNN_BLOCKS):
        lru_gates(i)
        gate_chunk(GATE_RNN, i)
    steps = BLOCK // N_CHUNKS
    for c in range(N_CHUNKS):
        scan_steps(c * steps, (c + 1) * steps)
        attn_proj(c)
    for c in range(NG):
        hst_scr[c] = hs[c]
    for hf in range(2):
        tail(hf)

    k2_scr[:, :, 0:BLOCK, :] = k2_scr[:, :, BLOCK:2 * BLOCK, :]
    k2_scr[:, :, 2 * BLOCK:3 * BLOCK, :] = k2_scr[:, :, 3 * BLOCK:4 * BLOCK, :]
    not_row0 = lax.broadcasted_iota(jnp.int32, (BT, N_KV, BLOCK, LANES), 2) > 0
    zero = jnp.zeros((), BF16)
    v2_scr[:, :, 0:BLOCK, :] = jnp.where(not_row0, v2_scr[:, :, BLOCK:2 * BLOCK, :], zero)
    v2_scr[:, :, 2 * BLOCK:3 * BLOCK, :] = jnp.where(not_row0, v2_scr[:, :, 3 * BLOCK:4 * BLOCK, :], zero)
    xe_scr[:, 0:TAIL, :] = xe_scr[:, BLOCK:BLOCK + TAIL, :]


def _const_spec(shape):
    zeros = (0,) * len(shape)
    return pl.BlockSpec(shape, lambda g, s: zeros, pipeline_mode=pl.Buffered(1))


def kernel(x, c, positions, w_ada, b_ada, norm_g, w_in, attn_sinks, conv_w, conv_b, rg_wa, rg_ba, rg_wx, rg_bx, rg_lambda, w_attn_proj, w_rnn_proj, w_out, final_g):
    B, S, D = x.shape
    assert (D, S % BLOCK, B % BT) == (D_MODEL, 0, 0)
    assert w_in.shape == (1, D_MODEL, IN_WIDTH)
    assert BT == N_CHUNKS == N_KV == N_RNN_BLOCKS
    n_groups, n_steps, rows = B // BT, S // BLOCK, BT * BLOCK

    mod = pl.pallas_call(
        _mod_kernel,
        grid=(3,),
        in_specs=[pl.BlockSpec((B, D), lambda i: (0, 0)),
                  pl.BlockSpec((D, D), lambda i: (0, i)),
                  pl.BlockSpec((1, D), lambda i: (0, i))],
        out_specs=pl.BlockSpec((B, D), lambda i: (0, i)),
        out_shape=jax.ShapeDtypeStruct((B, 3 * D), F32),
        name="adaln_mod",
    )(c, w_ada[0], b_ada)

    e2 = jnp.asarray(_expansion_matrix(), BF16)
    row = lambda v: v.reshape(1, -1)
    operands = (
        attn_sinks[0],
        x,
        positions.reshape(n_groups, BT, S),
        mod.reshape(n_groups, BT, 3 * D),
        norm_g, row(final_g), e2,
        w_in[0].astype(BF16), conv_w[0], conv_b, rg_wa[0].astype(BF16), rg_ba,
        rg_wx[0].astype(BF16), rg_bx, rg_lambda,
        w_attn_proj[0].astype(BF16), w_rnn_proj[0].astype(BF16), w_out[0].astype(BF16),
    )
    in_specs = [
        pl.BlockSpec(memory_space=pltpu.SMEM),
        pl.BlockSpec((BT, BLOCK, D), lambda g, s: (g, s, 0)),
        pl.BlockSpec((1, BT, BLOCK), lambda g, s: (g, 0, s)),
        pl.BlockSpec((1, BT, 3 * D), lambda g, s: (g, 0, 0)),
    ] + [_const_spec(op.shape) for op in operands[4:]]

    scratch = [
        pltpu.VMEM((rows, D), BF16),
        pltpu.VMEM((BT, N_KV, 2 * BLOCK, LANES), BF16),
        pltpu.VMEM((BT, N_KV, 4 * BLOCK, LANES), BF16),
        pltpu.VMEM((BT, N_KV, 4 * BLOCK, LANES), BF16),
        pltpu.VMEM((BT, BLOCK, 4 * LANES), F32),
        pltpu.VMEM((2 * BLOCK, 4 * BLOCK), F32),
        pltpu.VMEM((rows, D), BF16),
        pltpu.VMEM((rows, D), BF16),
        pltpu.VMEM((rows, D), BF16),
        pltpu.VMEM((rows, D), BF16),
        pltpu.VMEM((BT, N_KV, 2 * BLOCK, 4 * BLOCK), BF16),
        pltpu.VMEM((rows, D), BF16),
        pltpu.VMEM((BT, PITCH, D), F32),
        pltpu.VMEM((rows, D), F32),
        pltpu.VMEM((NG, BT * PITCH, LANES), F32),
        pltpu.VMEM((NG, BT * PITCH, LANES), F32),
        pltpu.VMEM((NG, BT, LANES), F32),
        pltpu.VMEM((rows, D), BF16),
        pltpu.VMEM((rows, D), F32),
        pltpu.VMEM((rows, D), BF16),
    ]

    out = pl.pallas_call(
        _block_kernel,
        grid=(n_groups, n_steps),
        in_specs=in_specs,
        out_specs=pl.BlockSpec((BT, BLOCK, D), lambda g, s: (g, s, 0)),
        out_shape=jax.ShapeDtypeStruct((B, S, D), F32),
        scratch_shapes=scratch,
        compiler_params=pltpu.CompilerParams(
            dimension_semantics=("arbitrary", "arbitrary"),
            vmem_limit_bytes=VMEM_LIMIT),
        name="hybrid_block",
    )(*operands)
    return out
```
